```python
import math
import jax, jax.numpy as jnp
from jax import lax
import numpy as np

D_MODEL = 2048
BATCH = 4
SEQ = 2048
DEPTH = 1
DEC_BATCH = 128
DEC_SEQ = 4
PAST_LEN = 16384
PAGE_SIZE = 128

D_MIX = D_MODEL
D_POOL = D_MIX // 2
POOL_WINDOWS = (2, 4, 8, 16)
POOL_GROUP = D_POOL // len(POOL_WINDOWS)
MAX_WIN = max(POOL_WINDOWS)
POOL_STATE = MAX_WIN - 1
D_SGU = D_MIX - D_POOL
SGU_HEADS = 8
SGU_HEAD_DIM = D_SGU // SGU_HEADS
CHUNK = 128
D_IN = D_POOL + 2 * D_SGU
MEM_LEN = 256
MEM_HEADS = 4
MEM_HEAD_DIM = D_MODEL // MEM_HEADS
PEER_HEADS = 8
PEER_N_KEYS = 128
PEER_N_EXPERTS = PEER_N_KEYS * PEER_N_KEYS
PEER_QUERY_DIM = 256
PEER_HALF = PEER_QUERY_DIM // 2
PEER_TOPK = 16
PEER_BLOCK = 128
ALPHA = (2.0 * DEPTH) ** 0.25
BETA = (8.0 * DEPTH) ** -0.25
LN_EPS = 1e-5

kernel_name = "hybrid_pool_sgu_peer_mem_decode_step"


def layer_norm(x, g, b):
    xf = x.astype(jnp.float32)
    mu = jnp.mean(xf, axis=-1, keepdims=True)
    xc = xf - mu
    var = jnp.mean(xc * xc, axis=-1, keepdims=True)
    y = xc * lax.rsqrt(var + LN_EPS) * g.astype(jnp.float32) + b.astype(jnp.float32)
    return y.astype(x.dtype)


def pool_mix(a_ext, n_prev, start_pos, pool_w, pool_scale):
    T = a_ext.shape[1] - n_prev
    af = a_ext.astype(jnp.float32)
    csum = jnp.pad(jnp.cumsum(af, axis=1), ((0, 0), (MAX_WIN, 0), (0, 0)))
    pos = start_pos + jnp.arange(T)
    outs = []
    for g, w in enumerate(POOL_WINDOWS):
        c0, c1 = g * POOL_GROUP, (g + 1) * POOL_GROUP
        hi = csum[:, n_prev + MAX_WIN:n_prev + MAX_WIN + T, c0:c1]
        lo = csum[:, n_prev + MAX_WIN - w:n_prev + MAX_WIN - w + T, c0:c1]
        cnt = jnp.minimum(pos + 1, w).astype(jnp.float32)[None, :, None]
        d = (hi - lo) / cnt - af[:, n_prev:, c0:c1]
        outs.append(jnp.einsum('btc,cd->btd', d.astype(a_ext.dtype), pool_w[g]))
    return jnp.concatenate(outs, axis=-1) * pool_scale


def sgu_mix(u, v_n, sgu_w, sgu_b):
    b_, T, _ = v_n.shape
    pad = (-T) % CHUNK
    vp = jnp.pad(v_n, ((0, 0), (0, pad), (0, 0))).reshape(b_, (T + pad) // CHUNK, CHUNK, SGU_HEADS, SGU_HEAD_DIM)
    mask = jnp.tril(jnp.ones((CHUNK, CHUNK), dtype=bool))
    w = jnp.where(mask[None], sgu_w, 0)
    s = jnp.einsum('hij,bcjhd->bcihd', w, vp) + jnp.transpose(sgu_b)[None, None, :, :, None]
    s = s.reshape(b_, T + pad, D_SGU)[:, :T]
    return u * s


def mem_kv(mem, w_mk, w_mv):
    b_, m, _ = mem.shape
    k = (mem @ w_mk).reshape(b_, m, MEM_HEADS, MEM_HEAD_DIM)
    v = (mem @ w_mv).reshape(b_, m, MEM_HEADS, MEM_HEAD_DIM)
    return k, v


def mem_attend(h, mem_k, mem_v, w_mq, w_mo):
    b_, T, _ = h.shape
    q = (h @ w_mq).reshape(b_, T, MEM_HEADS, MEM_HEAD_DIM)
    s = jnp.einsum('bthd,bmhd->bhtm', q, mem_k).astype(jnp.float32) * (MEM_HEAD_DIM ** -0.5)
    p = jax.nn.softmax(s, axis=-1).astype(mem_v.dtype)
    o = jnp.einsum('bhtm,bmhd->bthd', p, mem_v).reshape(b_, T, D_MODEL)
    return o @ w_mo


def peer_ffn(h, peer_wq, peer_subkeys, peer_u, peer_v):
    b_, T, D = h.shape
    n = b_ * T
    pad = (-n) % PEER_BLOCK
    blocks = jnp.pad(h.reshape(n, D), ((0, pad), (0, 0))).reshape(-1, PEER_BLOCK, D)

    def block(xb):
        q = (xb @ peer_wq).reshape(PEER_BLOCK, PEER_HEADS, 2, PEER_HALF)
        s = jnp.einsum('thpd,pkd->thpk', q, peer_subkeys).astype(jnp.float32)
        sv, si = lax.top_k(s, PEER_TOPK)
        cand = (sv[:, :, 0, :, None] + sv[:, :, 1, None, :]).reshape(PEER_BLOCK, PEER_HEADS, PEER_TOPK * PEER_TOPK)
        cidx = (si[:, :, 0, :, None] * PEER_N_KEYS + si[:, :, 1, None, :]).reshape(PEER_BLOCK, PEER_HEADS, PEER_TOPK * PEER_TOPK)
        fv, fi = lax.top_k(cand, PEER_TOPK)
        eidx = jnp.take_along_axis(cidx, fi, axis=-1)
        gate = jax.nn.softmax(fv, axis=-1).astype(xb.dtype)
        act = jax.nn.gelu(jnp.einsum('thkd,td->thk', peer_u[eidx], xb))
        return jnp.einsum('thk,thkd->td', gate * act, peer_v[eidx])

    out = lax.map(block, blocks).reshape(-1, D)[:n]
    return out.reshape(b_, T, D)


def decoder_layer(h, pool_prev, start_pos, mem_k, mem_v, w_in, pool_w, pool_scale, sgu_ln_g, sgu_ln_b,
                  sgu_w, sgu_b, w_out, ln1_g, ln1_b, w_mq, w_mo, ln2_g, ln2_b,
                  peer_wq, peer_subkeys, peer_u, peer_v, ln3_g, ln3_b):
    z = h @ w_in
    a = z[..., :D_POOL]
    uv = jax.nn.gelu(z[..., D_POOL:])
    u = uv[..., :D_SGU]
    v_n = layer_norm(uv[..., D_SGU:], sgu_ln_g, sgu_ln_b)
    if pool_prev is None:
        a_ext, n_prev = a, 0
    else:
        a_ext, n_prev = jnp.concatenate([pool_prev.astype(a.dtype), a], axis=1), POOL_STATE
    y_pool = pool_mix(a_ext, n_prev, start_pos, pool_w, pool_scale)
    y_sgu = sgu_mix(u, v_n, sgu_w, sgu_b)
    m = jnp.concatenate([y_pool, y_sgu], axis=-1) @ w_out
    h = layer_norm(ALPHA * h + m, ln1_g, ln1_b)
    h = layer_norm(ALPHA * h + mem_attend(h, mem_k, mem_v, w_mq, w_mo), ln2_g, ln2_b)
    h = layer_norm(ALPHA * h + peer_ffn(h, peer_wq, peer_subkeys, peer_u, peer_v), ln3_g, ln3_b)
    return h, a_ext[:, -POOL_STATE:], v_n


def setup_inputs(seed: int = 0) -> dict:
    key = jax.random.key(seed)
    ks = jax.random.split(key, 32)

    def nrm(k, shape, scale=1.0):
        return jax.random.normal(k, shape, dtype=jnp.float32) * scale

    L, D = DEPTH, D_MODEL
    return {
        'x_prompt': nrm(ks[0], (BATCH, SEQ, D)),
        'x_sample': nrm(ks[1], (DEC_BATCH, DEC_SEQ, D)),
        'mem_prompt': nrm(ks[2], (BATCH, MEM_LEN, D)),
        'cache_pool': nrm(ks[3], (L, DEC_BATCH, POOL_STATE, D_POOL)),
        'cache_mem_k': nrm(ks[4], (L, DEC_BATCH, MEM_LEN, MEM_HEADS, MEM_HEAD_DIM)),
        'cache_mem_v': nrm(ks[5], (L, DEC_BATCH, MEM_LEN, MEM_HEADS, MEM_HEAD_DIM), BETA),
        'w_in': nrm(ks[6], (L, D, D_IN), D ** -0.5),
        'pool_w': nrm(ks[7], (L, len(POOL_WINDOWS), POOL_GROUP, POOL_GROUP), POOL_GROUP ** -0.5),
        'pool_scale': 1.0 + nrm(ks[8], (L, D_POOL), 0.02),
        'sgu_ln_g': 1.0 + nrm(ks[9], (L, D_SGU), 0.02),
        'sgu_ln_b': nrm(ks[10], (L, D_SGU), 0.02),
        'sgu_w': nrm(ks[11], (L, SGU_HEADS, CHUNK, CHUNK), CHUNK ** -0.5),
        'sgu_b': 1.0 + nrm(ks[12], (L, SGU_HEADS, CHUNK), 0.02),
        'w_out': nrm(ks[13], (L, D_MIX, D), BETA * D_MIX ** -0.5),
        'ln1_g': 1.0 + nrm(ks[14], (L, D), 0.02),
        'ln1_b': nrm(ks[15], (L, D), 0.02),
        'w_mq': nrm(ks[16], (L, D, D), D ** -0.5),
        'w_mk': nrm(ks[17], (L, D, D), D ** -0.5),
        'w_mv': nrm(ks[18], (L, D, D), BETA * D ** -0.5),
        'w_mo': nrm(ks[19], (L, D, D), BETA * D ** -0.5),
        'ln2_g': 1.0 + nrm(ks[20], (L, D), 0.02),
        'ln2_b': nrm(ks[21], (L, D), 0.02),
        'peer_wq': nrm(ks[22], (L, D, PEER_HEADS * PEER_QUERY_DIM), D ** -0.5),
        'peer_subkeys': nrm(ks[23], (L, 2, PEER_N_KEYS, PEER_HALF), PEER_HALF ** -0.5),
        'peer_u': nrm(ks[24], (L, PEER_N_EXPERTS, D), D ** -0.5),
        'peer_v': nrm(ks[25], (L, PEER_N_EXPERTS, D), BETA),
        'ln3_g': 1.0 + nrm(ks[26], (L, D), 0.02),
        'ln3_b': nrm(ks[27], (L, D), 0.02),
    }


def reference(x_prompt, x_sample, mem_prompt, cache_pool, cache_mem_k, cache_mem_v, w_in, pool_w, pool_scale,
              sgu_ln_g, sgu_ln_b, sgu_w, sgu_b, w_out, ln1_g, ln1_b, w_mq, w_mk, w_mv, w_mo, ln2_g, ln2_b,
              peer_wq, peer_subkeys, peer_u, peer_v, ln3_g, ln3_b):
    hp, hs = x_prompt, x_sample
    pool_p_list, mk_list, mv_list, pool_s_list, vs_list = [], [], [], [], []
    for l in range(DEPTH):
        lw = (w_in[l], pool_w[l], pool_scale[l], sgu_ln_g[l], sgu_ln_b[l], sgu_w[l], sgu_b[l], w_out[l],
              ln1_g[l], ln1_b[l], w_mq[l], w_mo[l], ln2_g[l], ln2_b[l],
              peer_wq[l], peer_subkeys[l], peer_u[l], peer_v[l], ln3_g[l], ln3_b[l])
        mk, mv = mem_kv(mem_prompt, w_mk[l], w_mv[l])
        hp, pool_p, _ = decoder_layer(hp, None, 0, mk, mv, *lw)
        hs, pool_s, v_s = decoder_layer(hs, cache_pool[l], PAST_LEN, cache_mem_k[l], cache_mem_v[l], *lw)
        pool_p_list.append(pool_p)
        mk_list.append(mk)
        mv_list.append(mv)
        pool_s_list.append(pool_s)
        vs_list.append(v_s)
    state_pool_prompt = jnp.stack(pool_p_list)
    state_mem_k_prompt = jnp.stack(mk_list)
    state_mem_v_prompt = jnp.stack(mv_list)
    state_pool_sample = jnp.stack(pool_s_list)
    state_sgu_v_sample = jnp.stack(vs_list)
    return (hp, hs, state_pool_prompt, state_mem_k_prompt, state_mem_v_prompt, state_pool_sample, state_sgu_v_sample)
```

```python
import functools
import math

import jax
import jax.numpy as jnp
from jax import lax
from jax.experimental import pallas as pl
from jax.experimental.pallas import tpu as pltpu

F32 = jnp.float32
BF16 = jnp.bfloat16

V7X_LANES = 128
V7X_VMEM_LIMIT_BYTES = 56 * 1024 * 1024

POOL_WINDOWS = (2, 4, 8, 16)
MAX_WIN = max(POOL_WINDOWS)
POOL_STATE = MAX_WIN - 1
SGU_HEADS = 8
CHUNK = 128
MEM_HEADS = 4
PEER_HEADS = 8
PEER_N_KEYS = 128
PEER_TOPK = 16
PAST_LEN = 16384
DEPTH = 1
ALPHA = (2.0 * DEPTH) ** 0.25
LN_EPS = 1e-5

_PAIRS = tuple((a, b) for a in range(PEER_TOPK) for b in range(PEER_TOPK)
               if (a + 1) * (b + 1) <= PEER_TOPK)


def _cparams(semantics=None):
    return pltpu.CompilerParams(dimension_semantics=semantics,
                                vmem_limit_bytes=V7X_VMEM_LIMIT_BYTES)


def _ln(x, g, b):
    mu = jnp.mean(x, axis=-1, keepdims=True)
    xc = x - mu
    var = jnp.mean(xc * xc, axis=-1, keepdims=True)
    return xc * lax.rsqrt(var + LN_EPS) * g + b


def _gelu(x):
    return jax.nn.gelu(x, approximate=True)


def _mm_body(x_ref, w_ref, o_ref):
    o_ref[...] = jnp.dot(x_ref[...].astype(BF16), w_ref[...].astype(BF16),
                         preferred_element_type=F32).astype(o_ref.dtype)


def _matmul(x, w, out_dtype, tm, tn, name):
    m, k = x.shape
    _, n = w.shape
    assert m % tm == 0 and n % tn == 0
    return pl.pallas_call(
        _mm_body,
        grid=(m // tm, n // tn),
        in_specs=[pl.BlockSpec((tm, k), lambda i, j: (i, 0)),
                  pl.BlockSpec((k, tn), lambda i, j: (0, j))],
        out_specs=pl.BlockSpec((tm, tn), lambda i, j: (i, j)),
        out_shape=jax.ShapeDtypeStruct((m, n), out_dtype),
        compiler_params=_cparams(("arbitrary", "arbitrary")),
        name=name,
    )(x, w)


def _mm_ln_body(x_ref, w_ref, r_ref, g_ref, b_ref, o_ref):
    m = jnp.dot(x_ref[...].astype(BF16), w_ref[...], preferred_element_type=F32)
    o_ref[...] = _ln(ALPHA * r_ref[...] + m, g_ref[...], b_ref[...])


def _matmul_ln(x, w, resid, g, b, tm, name):
    m, k = x.shape
    _, n = w.shape
    assert m % tm == 0
    return pl.pallas_call(
        _mm_ln_body,
        grid=(m // tm,),
        in_specs=[pl.BlockSpec((tm, k), lambda i: (i, 0)),
                  pl.BlockSpec((k, n), lambda i: (0, 0)),
                  pl.BlockSpec((tm, n), lambda i: (i, 0)),
                  pl.BlockSpec((1, n), lambda i: (0, 0)),
                  pl.BlockSpec((1, n), lambda i: (0, 0))],
        out_specs=pl.BlockSpec((tm, n), lambda i: (i, 0)),
        out_shape=jax.ShapeDtypeStruct((m, n), F32),
        compiler_params=_cparams(("arbitrary",)),
        name=name,
    )(x, w, resid, g.reshape(1, n), b.reshape(1, n))


def _mix_prompt_body(a_ref, ap_ref, u_ref, v_ref, pw_ref, ps_ref, lg_ref, lb_ref, sw_ref, sb_ref,
                     cat_ref, aext_ref):
    c = pl.program_id(1)
    d_pool = a_ref.shape[1]
    grp = d_pool // len(POOL_WINDOWS)
    a = a_ref[...]
    aext_ref[0:MAX_WIN, :] = jnp.where(c > 0, ap_ref[...], 0.0)
    aext_ref[MAX_WIN:MAX_WIN + CHUNK, :] = a
    pos = lax.broadcasted_iota(jnp.int32, (CHUNK, grp), 0) + c * CHUNK
    for g, w in enumerate(POOL_WINDOWS):
        cols = slice(g * grp, (g + 1) * grp)
        s = aext_ref[MAX_WIN:MAX_WIN + CHUNK, cols]
        for k in range(1, w):
            s = s + aext_ref[MAX_WIN - k:MAX_WIN - k + CHUNK, cols]
        cnt = jnp.minimum(pos + 1, w).astype(F32)
        d = s / cnt - a[:, cols]
        y = jnp.dot(d.astype(BF16), pw_ref[g], preferred_element_type=F32) * ps_ref[:, cols]
        cat_ref[:, cols] = y.astype(cat_ref.dtype)
    u = _gelu(u_ref[...])
    vn = _ln(_gelu(v_ref[...]), lg_ref[...], lb_ref[...])
    hd = u.shape[1] // SGU_HEADS
    tri = (lax.broadcasted_iota(jnp.int32, (CHUNK, CHUNK), 0)
           >= lax.broadcasted_iota(jnp.int32, (CHUNK, CHUNK), 1))
    for h in range(SGU_HEADS):
        cols = slice(h * hd, (h + 1) * hd)
        wm = jnp.where(tri, sw_ref[h], 0.0).astype(BF16)
        s = jnp.dot(wm, vn[:, cols].astype(BF16), preferred_element_type=F32) + sb_ref[:, cols]
        cat_ref[:, d_pool + h * hd:d_pool + (h + 1) * hd] = (u[:, cols] * s).astype(cat_ref.dtype)


def _mix_prompt(z, batch, seq, d_pool, d_sgu, pool_w, pool_scale, ln_g, ln_b, sgu_w, sgu_bias):
    assert d_pool == d_sgu and seq % CHUNK == 0
    nch = seq // CHUNK
    sub = CHUNK // MAX_WIN
    row = lambda b, c: b * nch + c
    return pl.pallas_call(
        _mix_prompt_body,
        grid=(batch, nch),
        in_specs=[
            pl.BlockSpec((CHUNK, d_pool), lambda b, c: (row(b, c), 0)),
            pl.BlockSpec((MAX_WIN, d_pool), lambda b, c: (jnp.maximum(row(b, c) * sub - 1, 0), 0)),
            pl.BlockSpec((CHUNK, d_sgu), lambda b, c: (row(b, c), 1)),
            pl.BlockSpec((CHUNK, d_sgu), lambda b, c: (row(b, c), 2)),
            pl.BlockSpec(pool_w.shape, lambda b, c: (0, 0, 0)),
            pl.BlockSpec((1, d_pool), lambda b, c: (0, 0)),
            pl.BlockSpec((1, d_sgu), lambda b, c: (0, 0)),
            pl.BlockSpec((1, d_sgu), lambda b, c: (0, 0)),
            pl.BlockSpec(sgu_w.shape, lambda b, c: (0, 0, 0)),
            pl.BlockSpec((CHUNK, d_sgu), lambda b, c: (0, 0)),
        ],
        out_specs=pl.BlockSpec((CHUNK, d_pool + d_sgu), lambda b, c: (row(b, c), 0)),
        out_shape=jax.ShapeDtypeStruct((batch * seq, d_pool + d_sgu), BF16),
        scratch_shapes=[pltpu.VMEM((MAX_WIN + CHUNK, d_pool), F32)],
        compiler_params=_cparams(("arbitrary", "arbitrary")),
        name="mix_prompt",
    )(z, z, z, z, pool_w, pool_scale.reshape(1, d_pool), ln_g.reshape(1, d_sgu),
      ln_b.reshape(1, d_sgu), sgu_w, sgu_bias)


def _mix_sample_body(z_ref, cache_ref, pw_ref, ps_ref, lg_ref, lb_ref, wrow_ref, brow_ref,
                     cat_ref, vn_ref, st_ref, aext_ref, *, start_pos):
    t_new = z_ref.shape[0]
    d_pool = cache_ref.shape[2]
    d_sgu = vn_ref.shape[2]
    grp = d_pool // len(POOL_WINDOWS)
    aext_ref[0:POOL_STATE] = cache_ref[...]
    for t in range(t_new):
        aext_ref[POOL_STATE + t] = z_ref[t, :, 0:d_pool]
    st_ref[...] = aext_ref[t_new:t_new + POOL_STATE]
    for t in range(t_new):
        a_t = aext_ref[POOL_STATE + t]
        for g, w in enumerate(POOL_WINDOWS):
            cols = slice(g * grp, (g + 1) * grp)
            s = aext_ref[POOL_STATE + t, :, cols]
            for k in range(1, w):
                s = s + aext_ref[POOL_STATE + t - k, :, cols]
            cnt = float(min(start_pos + t + 1, w))
            d = s / cnt - a_t[:, cols]
            y = jnp.dot(d.astype(BF16), pw_ref[g], preferred_element_type=F32) * ps_ref[:, cols]
            cat_ref[t, :, cols] = y.astype(cat_ref.dtype)
    for t in range(t_new):
        vn_ref[t] = _ln(_gelu(z_ref[t, :, d_pool + d_sgu:d_pool + 2 * d_sgu]), lg_ref[...], lb_ref[...])
    for t in range(t_new):
        s = brow_ref[t:t + 1, :]
        for j in range(t + 1):
            s = s + wrow_ref[t, j:j + 1, :] * vn_ref[j]
        u = _gelu(z_ref[t, :, d_pool:d_pool + d_sgu])
        cat_ref[t, :, d_pool:d_pool + d_sgu] = (u * s).astype(cat_ref.dtype)


def _mix_sample(z3, cache_t, pool_w, pool_scale, ln_g, ln_b, wrow, brow, start_pos):
    t_new, nb, d_in = z3.shape
    d_pool = cache_t.shape[2]
    d_sgu = (d_in - d_pool) // 2
    return pl.pallas_call(
        functools.partial(_mix_sample_body, start_pos=start_pos),
        out_shape=(jax.ShapeDtypeStruct((t_new, nb, d_pool + d_sgu), BF16),
                   jax.ShapeDtypeStruct((t_new, nb, d_sgu), F32),
                   jax.ShapeDtypeStruct((POOL_STATE, nb, d_pool), F32)),
        scratch_shapes=[pltpu.VMEM((POOL_STATE + t_new, nb, d_pool), F32)],
        compiler_params=_cparams(),
        name="mix_sample",
    )(z3, cache_t, pool_w, pool_scale.reshape(1, d_pool), ln_g.reshape(1, d_sgu),
      ln_b.reshape(1, d_sgu), wrow, brow)


def _attn_body(q_ref, k_ref, v_ref, o_ref):
    hd = q_ref.shape[2] // MEM_HEADS
    scale = hd ** -0.5
    for h in range(MEM_HEADS):
        cols = slice(h * hd, (h + 1) * hd)
        qh = q_ref[0, :, cols].astype(BF16)
        kh = k_ref[0, :, cols].astype(BF16)
        s = lax.dot_general(qh, kh, (((1,), (1,)), ((), ())), preferred_element_type=F32) * scale
        e = jnp.exp(s - jnp.max(s, axis=-1, keepdims=True))
        p = e / jnp.sum(e, axis=-1, keepdims=True)
        o = jnp.dot(p.astype(BF16), v_ref[0, :, cols].astype(BF16), preferred_element_type=F32)
        o_ref[0, :, cols] = o.astype(o_ref.dtype)


def _attention(q, k, v, tq, name):
    nb, t, d = q.shape
    m = k.shape[1]
    assert t % tq == 0
    return pl.pallas_call(
        _attn_body,
        grid=(nb, t // tq),
        in_specs=[pl.BlockSpec((1, tq, d), lambda b, i: (b, i, 0)),
                  pl.BlockSpec((1, m, d), lambda b, i: (b, 0, 0)),
                  pl.BlockSpec((1, m, d), lambda b, i: (b, 0, 0))],
        out_specs=pl.BlockSpec((1, tq, d), lambda b, i: (b, i, 0)),
        out_shape=jax.ShapeDtypeStruct((nb, t, d), BF16),
        compiler_params=_cparams(("arbitrary", "arbitrary")),
        name=name,
    )(q, k, v)


def _peer_select_body(qt_ref, sk_ref, rank2_ref, nbk_ref, g1_ref, g2_ref,
                      s_scr, rank_scr, sv_scr, nb_scr, z_scr):
    nk = PEER_N_KEYS
    assert qt_ref.shape[1] == V7X_LANES
    neg = float("-inf")
    kio = lax.broadcasted_iota(jnp.int32, (nk, V7X_LANES), 0).astype(F32)
    for h in range(PEER_HEADS):
        for p in range(2):
            row0 = h * (2 * nk) + p * nk
            s = jnp.dot(sk_ref[p], qt_ref[row0:row0 + nk, :], preferred_element_type=F32)
            s_scr[2 * h + p] = s
            rem = s
            rank = jnp.full((nk, V7X_LANES), float(PEER_TOPK), F32)
            for r in range(PEER_TOPK):
                m = jnp.max(rem, axis=0, keepdims=True)
                first = jnp.min(jnp.where(rem == m, kio, float(nk)), axis=0, keepdims=True)
                hit = kio == first
                rank = jnp.where(hit, float(r), rank)
                rem = jnp.where(hit, neg, rem)
                sv_scr[p, r, h:h + 1, :] = m
            rank_scr[2 * h + p] = rank

    sv1 = [sv_scr[0, a] for a in range(PEER_TOPK)]
    sv2 = [sv_scr[1, b] for b in range(PEER_TOPK)]
    cand = {c: sv1[c[0]] + sv2[c[1]] for c in _PAIRS}
    rk = {c: jnp.full((PEER_HEADS, V7X_LANES), float(i), F32) for i, c in enumerate(_PAIRS)}
    for i, c in enumerate(_PAIRS):
        for c2 in _PAIRS[i + 1:]:
            if c2[0] >= c[0] and c2[1] >= c[1]:
                continue
            later_wins = jnp.where(cand[c2] > cand[c], 1.0, 0.0)
            rk[c] = rk[c] + later_wins
            rk[c2] = rk[c2] - later_wins
    mx = cand[(0, 0)]
    zsum = jnp.zeros((PEER_HEADS, V7X_LANES), F32)
    nb = [jnp.zeros((PEER_HEADS, V7X_LANES), F32) for _ in range(PEER_TOPK)]
    for c in _PAIRS:
        sel = rk[c] < float(PEER_TOPK)
        nb[c[0]] = nb[c[0]] + jnp.where(sel, 1.0, 0.0)
        zsum = zsum + jnp.where(sel, jnp.exp(cand[c] - mx), 0.0)
    for a in range(PEER_TOPK):
        nb_scr[a] = nb[a]
    z_scr[...] = zsum

    for h in range(PEER_HEADS):
        r1 = rank_scr[2 * h]
        nbk = jnp.zeros((nk, V7X_LANES), F32)
        for a in range(PEER_TOPK):
            nbk = jnp.where(r1 == float(a), nb_scr[a, h:h + 1, :], nbk)
        nbk_ref[h] = nbk
        rank2_ref[h] = rank_scr[2 * h + 1]
        g1_ref[h] = jnp.exp(s_scr[2 * h] - sv_scr[0, 0, h:h + 1, :])
        g2_ref[h] = jnp.exp(s_scr[2 * h + 1] - sv_scr[1, 0, h:h + 1, :]) / z_scr[h:h + 1, :]


def _peer_select(qt, subkeys, tl):
    rows, n = qt.shape
    assert rows == PEER_HEADS * 2 * PEER_N_KEYS and n % tl == 0
    out = jax.ShapeDtypeStruct((PEER_HEADS, PEER_N_KEYS, n), F32)
    ospec = pl.BlockSpec((PEER_HEADS, PEER_N_KEYS, tl), lambda i: (0, 0, i))
    return pl.pallas_call(
        _peer_select_body,
        grid=(n // tl,),
        in_specs=[pl.BlockSpec((rows, tl), lambda i: (0, i)),
                  pl.BlockSpec(subkeys.shape, lambda i: (0, 0, 0))],
        out_specs=(ospec, ospec, ospec, ospec),
        out_shape=(out, out, out, out),
        scratch_shapes=[pltpu.VMEM((2 * PEER_HEADS, PEER_N_KEYS, V7X_LANES), F32),
                        pltpu.VMEM((2 * PEER_HEADS, PEER_N_KEYS, V7X_LANES), F32),
                        pltpu.VMEM((2, PEER_TOPK, PEER_HEADS, V7X_LANES), F32),
                        pltpu.VMEM((PEER_TOPK, PEER_HEADS, V7X_LANES), F32),
                        pltpu.VMEM((PEER_HEADS, V7X_LANES), F32)],
        compiler_params=_cparams(("arbitrary",)),
        name="peer_select",
    )(qt, subkeys)


def _peer_dense_body(u_ref, vt_ref, xt_ref, rank2_ref, nbk_ref, g1_ref, g2_ref, o_ref, a_scr, g_scr):
    j = pl.program_id(1)
    te = u_ref.shape[0]
    t = xt_ref.shape[1]
    nk = PEER_N_KEYS

    @pl.when(j == 0)
    def _():
        o_ref[...] = jnp.zeros_like(o_ref)

    a_scr[...] = _gelu(jnp.dot(u_ref[...], xt_ref[...], preferred_element_type=F32))
    for r in range(te // nk):
        rows = slice(r * nk, (r + 1) * nk)
        for lc in range(t // V7X_LANES):
            lanes = slice(lc * V7X_LANES, (lc + 1) * V7X_LANES)
            coef = jnp.zeros((nk, V7X_LANES), F32)
            for h in range(PEER_HEADS):
                sel = rank2_ref[h, :, lanes] < nbk_ref[h, r:r + 1, lanes]
                gate = g1_ref[h, r:r + 1, lanes] * g2_ref[h, :, lanes]
                coef = coef + jnp.where(sel, gate, 0.0)
            g_scr[rows, lanes] = (coef * a_scr[rows, lanes]).astype(BF16)
    o_ref[...] += jnp.dot(vt_ref[...], g_scr[...], preferred_element_type=F32)


def _peer_dense(u, vt, xt, rank2, nbk, g1, g2, t, te):
    e, d = u.shape
    n = xt.shape[1]
    assert n % t == 0 and e % te == 0 and te % PEER_N_KEYS == 0
    kpt = te // PEER_N_KEYS
    sspec = pl.BlockSpec((PEER_HEADS, PEER_N_KEYS, t), lambda i, j: (0, 0, i))
    rspec = pl.BlockSpec((PEER_HEADS, kpt, t), lambda i, j: (0, j, i))
    return pl.pallas_call(
        _peer_dense_body,
        grid=(n // t, e // te),
        in_specs=[pl.BlockSpec((te, d), lambda i, j: (j, 0)),
                  pl.BlockSpec((d, te), lambda i, j: (0, j)),
                  pl.BlockSpec((d, t), lambda i, j: (0, i)),
                  sspec, rspec, rspec, sspec],
        out_specs=pl.BlockSpec((d, t), lambda i, j: (0, i)),
        out_shape=jax.ShapeDtypeStruct((d, n), F32),
        scratch_shapes=[pltpu.VMEM((te, t), F32), pltpu.VMEM((te, t), BF16)],
        compiler_params=_cparams(("arbitrary", "arbitrary")),
        name="peer_dense",
    )(u, vt, xt, rank2, nbk, g1, g2)


def _ln_t_body(ot_ref, r_ref, g_ref, b_ref, y_ref):
    y_ref[...] = _ln(ALPHA * r_ref[...] + ot_ref[...].T, g_ref[...], b_ref[...])


def _ln_transposed(ot, resid, g, b, tm):
    d, n = ot.shape
    assert n % tm == 0
    return pl.pallas_call(
        _ln_t_body,
        grid=(n // tm,),
        in_specs=[pl.BlockSpec((d, tm), lambda i: (0, i)),
                  pl.BlockSpec((tm, d), lambda i: (i, 0)),
                  pl.BlockSpec((1, d), lambda i: (0, 0)),
                  pl.BlockSpec((1, d), lambda i: (0, 0))],
        out_specs=pl.BlockSpec((tm, d), lambda i: (i, 0)),
        out_shape=jax.ShapeDtypeStruct((n, d), F32),
        compiler_params=_cparams(("arbitrary",)),
        name="ln_transposed",
    )(ot, resid, g.reshape(1, d), b.reshape(1, d))


def _layer_tail(h0, cat, attend, wts, tag):
    n = h0.shape[0]
    tm = min(256, n)
    h1 = _matmul_ln(cat, wts["w_out"], h0, wts["ln1_g"], wts["ln1_b"], tm, "out_proj_ln1_" + tag)
    q = _matmul(h1, wts["w_mq"], F32, min(512, n), 1024, "mem_q_" + tag)
    o = attend(q)
    h2 = _matmul_ln(o, wts["w_mo"], h1, wts["ln2_g"], wts["ln2_b"], tm, "mem_o_ln2_" + tag)
    xt = h2.astype(BF16).T
    qt = _matmul(wts["peer_wq_t"], xt, BF16, 512, 512, "peer_q_" + tag)
    rank2, nbk, g1, g2 = _peer_select(qt, wts["subkeys"], V7X_LANES)
    ot = _peer_dense(wts["peer_u"], wts["peer_vt"], xt, rank2, nbk, g1, g2, 512, 1024)
    return _ln_transposed(ot, h2, wts["ln3_g"], wts["ln3_b"], 512)


def kernel(x_prompt, x_sample, mem_prompt, cache_pool, cache_mem_k, cache_mem_v, w_in, pool_w, pool_scale,
           sgu_ln_g, sgu_ln_b, sgu_w, sgu_b, w_out, ln1_g, ln1_b, w_mq, w_mk, w_mv, w_mo, ln2_g, ln2_b,
           peer_wq, peer_subkeys, peer_u, peer_v, ln3_g, ln3_b):
    assert w_in.shape[0] == DEPTH
    l = 0
    batch, seq, d = x_prompt.shape
    dec_b, dec_t, _ = x_sample.shape
    d_pool = pool_scale.shape[1]
    d_sgu = sgu_ln_g.shape[1]
    mem_len = mem_prompt.shape[1]
    mh, mhd = cache_mem_k.shape[3], cache_mem_k.shape[4]

    wts = dict(
        w_out=w_out[l].astype(BF16), ln1_g=ln1_g[l], ln1_b=ln1_b[l],
        w_mq=w_mq[l].astype(BF16), w_mo=w_mo[l].astype(BF16), ln2_g=ln2_g[l], ln2_b=ln2_b[l],
        peer_wq_t=peer_wq[l].astype(BF16).T, subkeys=peer_subkeys[l].astype(BF16),
        peer_u=peer_u[l].astype(BF16), peer_vt=peer_v[l].astype(BF16).T,
        ln3_g=ln3_g[l], ln3_b=ln3_b[l])
    w_in_b = w_in[l].astype(BF16)
    pool_w_b = pool_w[l].astype(BF16)
    hd = d_sgu // SGU_HEADS
    sgu_bias = jnp.repeat(sgu_b[l].T, hd, axis=1)

    hp0 = x_prompt.reshape(batch * seq, d)
    zp = _matmul(hp0, w_in_b, F32, 512, 1024, "in_proj_p")
    cat_p = _mix_prompt(zp, batch, seq, d_pool, d_sgu, pool_w_b, pool_scale[l], sgu_ln_g[l], sgu_ln_b[l],
                        sgu_w[l], sgu_bias)
    pool_p = zp.reshape(batch, seq, -1)[:, seq - POOL_STATE:, :d_pool]
    mem2 = mem_prompt.reshape(batch * mem_len, d)
    mk = _matmul(mem2, w_mk[l].astype(BF16), F32, 512, 1024, "mem_k")
    mv = _matmul(mem2, w_mv[l].astype(BF16), F32, 512, 1024, "mem_v")
    mk3 = mk.reshape(batch, mem_len, d)
    mv3 = mv.reshape(batch, mem_len, d)

    def attend_p(q):
        return _attention(q.reshape(batch, seq, d), mk3, mv3, 512, "attn_p").reshape(batch * seq, d)

    yp = _layer_tail(hp0, cat_p, attend_p, wts, "p")

    hs0 = jnp.transpose(x_sample, (1, 0, 2)).reshape(dec_t * dec_b, d)
    zs = _matmul(hs0, w_in_b, F32, dec_t * dec_b, 1024, "in_proj_s")
    cache_t = jnp.transpose(cache_pool[l], (1, 0, 2))
    wrow = jnp.repeat(jnp.transpose(sgu_w[l][:, :dec_t, :dec_t], (1, 2, 0)), hd, axis=2)
    brow = jnp.repeat(sgu_b[l][:, :dec_t].T, hd, axis=1)
    cat_s, vn_s, pool_s = _mix_sample(zs.reshape(dec_t, dec_b, -1), cache_t, pool_w_b, pool_scale[l],
                                      sgu_ln_g[l], sgu_ln_b[l], wrow, brow, PAST_LEN)
    ck = cache_mem_k[l].reshape(dec_b, mem_len, d)
    cv = cache_mem_v[l].reshape(dec_b, mem_len, d)

    def attend_s(q):
        qb = jnp.transpose(q.reshape(dec_t, dec_b, d), (1, 0, 2))
        ob = _attention(qb, ck, cv, dec_t, "attn_s")
        return jnp.transpose(ob, (1, 0, 2)).reshape(dec_t * dec_b, d)

    ys = _layer_tail(hs0, cat_s.reshape(dec_t * dec_b, -1), attend_s, wts, "s")

    y_prompt = yp.reshape(batch, seq, d)
    y_sample = jnp.transpose(ys.reshape(dec_t, dec_b, d), (1, 0, 2))
    state_pool_prompt = pool_p[None]
    state_mem_k_prompt = mk.reshape(1, batch, mem_len, mh, mhd)
    state_mem_v_prompt = mv.reshape(1, batch, mem_len, mh, mhd)
    state_pool_sample = jnp.transpose(pool_s, (1, 0, 2))[None]
    state_sgu_v_sample = jnp.transpose(vn_s, (1, 0, 2))[None]
    return (y_prompt, y_sample, state_pool_prompt, state_mem_k_prompt, state_mem_v_prompt,
            state_pool_sample, state_sgu_v_sample)
```

```python
import functools
import math

import jax
import jax.numpy as jnp
from jax import lax
from jax.experimental import pallas as pl
from jax.experimental.pallas import tpu as pltpu

F32 = jnp.float32
BF16 = jnp.bfloat16

V7X_LANES = 128
V7X_VMEM_LIMIT_BYTES = 56 * 1024 * 1024

POOL_WINDOWS = (2, 4, 8, 16)
MAX_WIN = max(POOL_WINDOWS)
POOL_STATE = MAX_WIN - 1
SGU_HEADS = 8
CHUNK = 128
MEM_HEADS = 4
PEER_HEADS = 8
PEER_N_KEYS = 128
PEER_TOPK = 16
PAST_LEN = 16384
DEPTH = 1
ALPHA = (2.0 * DEPTH) ** 0.25
LN_EPS = 1e-5

_PAIRS = tuple((a, b) for a in range(PEER_TOPK) for b in range(PEER_TOPK)
               if (a + 1) * (b + 1) <= PEER_TOPK)


def _cparams(semantics=None, flags=None):
    return pltpu.CompilerParams(dimension_semantics=semantics, flags=flags,
                                vmem_limit_bytes=V7X_VMEM_LIMIT_BYTES)


def _ln(x, g, b):
    mu = jnp.mean(x, axis=-1, keepdims=True)
    xc = x - mu
    var = jnp.mean(xc * xc, axis=-1, keepdims=True)
    return xc * lax.rsqrt(var + LN_EPS) * g + b


def _gelu(x):
    return jax.nn.gelu(x, approximate=True)


def _mm_body(x_ref, w_ref, o_ref):
    o_ref[...] = jnp.dot(x_ref[...].astype(BF16), w_ref[...].astype(BF16),
                         preferred_element_type=F32).astype(o_ref.dtype)


def _matmul(x, w, out_dtype, tm, tn, name):
    m, k = x.shape
    _, n = w.shape
    assert m % tm == 0 and n % tn == 0
    return pl.pallas_call(
        _mm_body,
        grid=(m // tm, n // tn),
        in_specs=[pl.BlockSpec((tm, k), lambda i, j: (i, 0)),
                  pl.BlockSpec((k, tn), lambda i, j: (0, j))],
        out_specs=pl.BlockSpec((tm, tn), lambda i, j: (i, j)),
        out_shape=jax.ShapeDtypeStruct((m, n), out_dtype),
        compiler_params=_cparams(("arbitrary", "arbitrary")),
        name=name,
    )(x, w)


def _mm_ln_body(x_ref, w_ref, r_ref, g_ref, b_ref, o_ref, *t_ref):
    m = jnp.dot(x_ref[...].astype(BF16), w_ref[...], preferred_element_type=F32)
    y = _ln(ALPHA * r_ref[...] + m, g_ref[...], b_ref[...])
    o_ref[...] = y
    if t_ref:
        t_ref[0][...] = y.T.astype(t_ref[0].dtype)


def _matmul_ln(x, w, resid, g, b, tm, name, with_transpose=False):
    m, k = x.shape
    _, n = w.shape
    assert m % tm == 0
    out_specs = pl.BlockSpec((tm, n), lambda i: (i, 0))
    out_shape = jax.ShapeDtypeStruct((m, n), F32)
    if with_transpose:
        out_specs = (out_specs, pl.BlockSpec((n, tm), lambda i: (0, i)))
        out_shape = (out_shape, jax.ShapeDtypeStruct((n, m), BF16))
    return pl.pallas_call(
        _mm_ln_body,
        grid=(m // tm,),
        in_specs=[pl.BlockSpec((tm, k), lambda i: (i, 0)),
                  pl.BlockSpec((k, n), lambda i: (0, 0)),
                  pl.BlockSpec((tm, n), lambda i: (i, 0)),
                  pl.BlockSpec((1, n), lambda i: (0, 0)),
                  pl.BlockSpec((1, n), lambda i: (0, 0))],
        out_specs=out_specs,
        out_shape=out_shape,
        compiler_params=_cparams(("arbitrary",)),
        name=name,
    )(x, w, resid, g.reshape(1, n), b.reshape(1, n))


def _cast_table_body(x_ref, o_ref, *, transpose):
    x = x_ref[0]
    o_ref[...] = (x.T if transpose else x).astype(o_ref.dtype)


def _cast_table(w, l, tr, transpose, name):
    _, e, d = w.shape
    assert e % tr == 0
    if transpose:
        out_spec, out_dims = pl.BlockSpec((d, tr), lambda i: (0, i)), (d, e)
    else:
        out_spec, out_dims = pl.BlockSpec((tr, d), lambda i: (i, 0)), (e, d)
    return pl.pallas_call(
        functools.partial(_cast_table_body, transpose=transpose),
        grid=(e // tr,),
        in_specs=[pl.BlockSpec((1, tr, d), lambda i: (l, i, 0))],
        out_specs=out_spec,
        out_shape=jax.ShapeDtypeStruct(out_dims, BF16),
        compiler_params=_cparams(("arbitrary",)),
        name=name,
    )(w)


def _mix_prompt_body(a_ref, ap_ref, u_ref, v_ref, pw_ref, ps_ref, lg_ref, lb_ref, sw_ref, sb_ref,
                     cat_ref, aext_ref):
    c = pl.program_id(1)
    d_pool = a_ref.shape[1]
    grp = d_pool // len(POOL_WINDOWS)
    a = a_ref[...]
    aext_ref[0:MAX_WIN, :] = jnp.where(c > 0, ap_ref[...], 0.0)
    aext_ref[MAX_WIN:MAX_WIN + CHUNK, :] = a
    pos = lax.broadcasted_iota(jnp.int32, (CHUNK, grp), 0) + c * CHUNK
    for g, w in enumerate(POOL_WINDOWS):
        cols = slice(g * grp, (g + 1) * grp)
        s = aext_ref[MAX_WIN:MAX_WIN + CHUNK, cols]
        for k in range(1, w):
            s = s + aext_ref[MAX_WIN - k:MAX_WIN - k + CHUNK, cols]
        cnt = jnp.minimum(pos + 1, w).astype(F32)
        d = s / cnt - a[:, cols]
        y = jnp.dot(d.astype(BF16), pw_ref[g], preferred_element_type=F32) * ps_ref[:, cols]
        cat_ref[:, cols] = y.astype(cat_ref.dtype)
    u = _gelu(u_ref[...])
    vn = _ln(_gelu(v_ref[...]), lg_ref[...], lb_ref[...])
    hd = u.shape[1] // SGU_HEADS
    tri = (lax.broadcasted_iota(jnp.int32, (CHUNK, CHUNK), 0)
           >= lax.broadcasted_iota(jnp.int32, (CHUNK, CHUNK), 1))
    for h in range(SGU_HEADS):
        cols = slice(h * hd, (h + 1) * hd)
        wm = jnp.where(tri, sw_ref[h], 0.0).astype(BF16)
        s = jnp.dot(wm, vn[:, cols].astype(BF16), preferred_element_type=F32) + sb_ref[:, cols]
        cat_ref[:, d_pool + h * hd:d_pool + (h + 1) * hd] = (u[:, cols] * s).astype(cat_ref.dtype)


def _mix_prompt(z, batch, seq, d_pool, d_sgu, pool_w, pool_scale, ln_g, ln_b, sgu_w, sgu_bias):
    assert d_pool == d_sgu and seq % CHUNK == 0
    nch = seq // CHUNK
    sub = CHUNK // MAX_WIN
    row = lambda b, c: b * nch + c
    return pl.pallas_call(
        _mix_prompt_body,
        grid=(batch, nch),
        in_specs=[
            pl.BlockSpec((CHUNK, d_pool), lambda b, c: (row(b, c), 0)),
            pl.BlockSpec((MAX_WIN, d_pool), lambda b, c: (jnp.maximum(row(b, c) * sub - 1, 0), 0)),
            pl.BlockSpec((CHUNK, d_sgu), lambda b, c: (row(b, c), 1)),
            pl.BlockSpec((CHUNK, d_sgu), lambda b, c: (row(b, c), 2)),
            pl.BlockSpec(pool_w.shape, lambda b, c: (0, 0, 0)),
            pl.BlockSpec((1, d_pool), lambda b, c: (0, 0)),
            pl.BlockSpec((1, d_sgu), lambda b, c: (0, 0)),
            pl.BlockSpec((1, d_sgu), lambda b, c: (0, 0)),
            pl.BlockSpec(sgu_w.shape, lambda b, c: (0, 0, 0)),
            pl.BlockSpec((CHUNK, d_sgu), lambda b, c: (0, 0)),
        ],
        out_specs=pl.BlockSpec((CHUNK, d_pool + d_sgu), lambda b, c: (row(b, c), 0)),
        out_shape=jax.ShapeDtypeStruct((batch * seq, d_pool + d_sgu), BF16),
        scratch_shapes=[pltpu.VMEM((MAX_WIN + CHUNK, d_pool), F32)],
        compiler_params=_cparams(("arbitrary", "arbitrary")),
        name="mix_prompt",
    )(z, z, z, z, pool_w, pool_scale.reshape(1, d_pool), ln_g.reshape(1, d_sgu),
      ln_b.reshape(1, d_sgu), sgu_w, sgu_bias)


def _mix_sample_body(z_ref, cache_ref, pw_ref, ps_ref, lg_ref, lb_ref, wrow_ref, brow_ref,
                     cat_ref, vn_ref, st_ref, aext_ref, *, start_pos):
    t_new = z_ref.shape[0]
    d_pool = cache_ref.shape[2]
    d_sgu = vn_ref.shape[2]
    grp = d_pool // len(POOL_WINDOWS)
    aext_ref[0:POOL_STATE] = cache_ref[...]
    for t in range(t_new):
        aext_ref[POOL_STATE + t] = z_ref[t, :, 0:d_pool]
    st_ref[...] = aext_ref[t_new:t_new + POOL_STATE]
    for t in range(t_new):
        a_t = aext_ref[POOL_STATE + t]
        for g, w in enumerate(POOL_WINDOWS):
            cols = slice(g * grp, (g + 1) * grp)
            s = aext_ref[POOL_STATE + t, :, cols]
            for k in range(1, w):
                s = s + aext_ref[POOL_STATE + t - k, :, cols]
            cnt = float(min(start_pos + t + 1, w))
            d = s / cnt - a_t[:, cols]
            y = jnp.dot(d.astype(BF16), pw_ref[g], preferred_element_type=F32) * ps_ref[:, cols]
            cat_ref[t, :, cols] = y.astype(cat_ref.dtype)
    for t in range(t_new):
        vn_ref[t] = _ln(_gelu(z_ref[t, :, d_pool + d_sgu:d_pool + 2 * d_sgu]), lg_ref[...], lb_ref[...])
    for t in range(t_new):
        s = brow_ref[t:t + 1, :]
        for j in range(t + 1):
            s = s + wrow_ref[t, j:j + 1, :] * vn_ref[j]
        u = _gelu(z_ref[t, :, d_pool:d_pool + d_sgu])
        cat_ref[t, :, d_pool:d_pool + d_sgu] = (u * s).astype(cat_ref.dtype)


def _mix_sample(z3, cache_t, pool_w, pool_scale, ln_g, ln_b, wrow, brow, start_pos):
    t_new, nb, d_in = z3.shape
    d_pool = cache_t.shape[2]
    d_sgu = (d_in - d_pool) // 2
    return pl.pallas_call(
        functools.partial(_mix_sample_body, start_pos=start_pos),
        out_shape=(jax.ShapeDtypeStruct((t_new, nb, d_pool + d_sgu), BF16),
                   jax.ShapeDtypeStruct((t_new, nb, d_sgu), F32),
                   jax.ShapeDtypeStruct((POOL_STATE, nb, d_pool), F32)),
        scratch_shapes=[pltpu.VMEM((POOL_STATE + t_new, nb, d_pool), F32)],
        compiler_params=_cparams(),
        name="mix_sample",
    )(z3, cache_t, pool_w, pool_scale.reshape(1, d_pool), ln_g.reshape(1, d_sgu),
      ln_b.reshape(1, d_sgu), wrow, brow)


def _attn_body(q_ref, k_ref, v_ref, o_ref):
    hd = q_ref.shape[2] // MEM_HEADS
    scale = hd ** -0.5
    for h in range(MEM_HEADS):
        cols = slice(h * hd, (h + 1) * hd)
        qh = q_ref[0, :, cols].astype(BF16)
        kh = k_ref[0, :, cols].astype(BF16)
        s = lax.dot_general(qh, kh, (((1,), (1,)), ((), ())), preferred_element_type=F32) * scale
        e = jnp.exp(s - jnp.max(s, axis=-1, keepdims=True))
        p = e / jnp.sum(e, axis=-1, keepdims=True)
        o = jnp.dot(p.astype(BF16), v_ref[0, :, cols].astype(BF16), preferred_element_type=F32)
        o_ref[0, :, cols] = o.astype(o_ref.dtype)


def _attention(q, k, v, tq, name):
    nb, t, d = q.shape
    m = k.shape[1]
    assert t % tq == 0
    return pl.pallas_call(
        _attn_body,
        grid=(nb, t // tq),
        in_specs=[pl.BlockSpec((1, tq, d), lambda b, i: (b, i, 0)),
                  pl.BlockSpec((1, m, d), lambda b, i: (b, 0, 0)),
                  pl.BlockSpec((1, m, d), lambda b, i: (b, 0, 0))],
        out_specs=pl.BlockSpec((1, tq, d), lambda b, i: (b, i, 0)),
        out_shape=jax.ShapeDtypeStruct((nb, t, d), BF16),
        compiler_params=_cparams(("arbitrary", "arbitrary")),
        name=name,
    )(q, k, v)


def _peer_select_body(qt_ref, sk_ref, rank2_ref, nbk_ref, g1_ref, g2_ref,
                      s_scr, rank_scr, sv_scr, nb_scr, z_scr):
    nk = PEER_N_KEYS
    assert qt_ref.shape[1] == V7X_LANES
    neg = float("-inf")
    kio = lax.broadcasted_iota(jnp.int32, (nk, V7X_LANES), 0).astype(F32)
    for h in range(PEER_HEADS):
        for p in range(2):
            row0 = h * (2 * nk) + p * nk
            s = jnp.dot(sk_ref[p], qt_ref[row0:row0 + nk, :], preferred_element_type=F32)
            s_scr[2 * h + p] = s
            rem = s
            rank = jnp.full((nk, V7X_LANES), float(PEER_TOPK), F32)
            for r in range(PEER_TOPK):
                m = jnp.max(rem, axis=0, keepdims=True)
                first = jnp.min(jnp.where(rem == m, kio, float(nk)), axis=0, keepdims=True)
                hit = kio == first
                rank = jnp.where(hit, float(r), rank)
                rem = jnp.where(hit, neg, rem)
                sv_scr[p, r, h:h + 1, :] = m
            rank_scr[2 * h + p] = rank

    sv1 = [sv_scr[0, a] for a in range(PEER_TOPK)]
    sv2 = [sv_scr[1, b] for b in range(PEER_TOPK)]
    cand = {c: sv1[c[0]] + sv2[c[1]] for c in _PAIRS}
    rk = {c: jnp.full((PEER_HEADS, V7X_LANES), float(i), F32) for i, c in enumerate(_PAIRS)}
    for i, c in enumerate(_PAIRS):
        for c2 in _PAIRS[i + 1:]:
            if c2[0] >= c[0] and c2[1] >= c[1]:
                continue
            later_wins = jnp.where(cand[c2] > cand[c], 1.0, 0.0)
            rk[c] = rk[c] + later_wins
            rk[c2] = rk[c2] - later_wins
    mx = cand[(0, 0)]
    zsum = jnp.zeros((PEER_HEADS, V7X_LANES), F32)
    nb = [jnp.zeros((PEER_HEADS, V7X_LANES), F32) for _ in range(PEER_TOPK)]
    for c in _PAIRS:
        sel = rk[c] < float(PEER_TOPK)
        nb[c[0]] = nb[c[0]] + jnp.where(sel, 1.0, 0.0)
        zsum = zsum + jnp.where(sel, jnp.exp(cand[c] - mx), 0.0)
    for a in range(PEER_TOPK):
        nb_scr[a] = nb[a]
    z_scr[...] = zsum

    for h in range(PEER_HEADS):
        r1 = rank_scr[2 * h]
        nbk = jnp.zeros((nk, V7X_LANES), F32)
        for a in range(PEER_TOPK):
            nbk = jnp.where(r1 == float(a), nb_scr[a, h:h + 1, :], nbk)
        nbk_ref[h] = nbk
        rank2_ref[h] = rank_scr[2 * h + 1]
        g1_ref[h] = jnp.exp(s_scr[2 * h] - sv_scr[0, 0, h:h + 1, :])
        g2_ref[h] = jnp.exp(s_scr[2 * h + 1] - sv_scr[1, 0, h:h + 1, :]) / z_scr[h:h + 1, :]


def _peer_select(qt, subkeys, tl):
    rows, n = qt.shape
    assert rows == PEER_HEADS * 2 * PEER_N_KEYS and n % tl == 0
    out = jax.ShapeDtypeStruct((PEER_HEADS, PEER_N_KEYS, n), F32)
    ospec = pl.BlockSpec((PEER_HEADS, PEER_N_KEYS, tl), lambda i: (0, 0, i))
    return pl.pallas_call(
        _peer_select_body,
        grid=(n // tl,),
        in_specs=[pl.BlockSpec((rows, tl), lambda i: (0, i)),
                  pl.BlockSpec(subkeys.shape, lambda i: (0, 0, 0))],
        out_specs=(ospec, ospec, ospec, ospec),
        out_shape=(out, out, out, out),
        scratch_shapes=[pltpu.VMEM((2 * PEER_HEADS, PEER_N_KEYS, V7X_LANES), F32),
                        pltpu.VMEM((2 * PEER_HEADS, PEER_N_KEYS, V7X_LANES), F32),
                        pltpu.VMEM((2, PEER_TOPK, PEER_HEADS, V7X_LANES), F32),
                        pltpu.VMEM((PEER_TOPK, PEER_HEADS, V7X_LANES), F32),
                        pltpu.VMEM((PEER_HEADS, V7X_LANES), F32)],
        compiler_params=_cparams(("arbitrary",)),
        name="peer_select",
    )(qt, subkeys)


def _peer_dense_stages(u_ref, vt_ref, xt_ref, rank2_ref, nbk_ref, g1_ref, g2_ref, o_ref, a_scr, g_scr):
    te = u_ref.shape[0]
    d = u_ref.shape[1]
    t = xt_ref.shape[1]
    nk = PEER_N_KEYS

    def stage1(c, n):
        rows = slice(c * (te // n), (c + 1) * (te // n))
        a_scr[rows, :] = _gelu(jnp.dot(u_ref[rows, :], xt_ref[...], preferred_element_type=F32))

    def stage2(r):
        rows = slice(r * nk, (r + 1) * nk)
        for lc in range(t // V7X_LANES):
            lanes = slice(lc * V7X_LANES, (lc + 1) * V7X_LANES)
            coef = jnp.zeros((nk, V7X_LANES), F32)
            for h in range(PEER_HEADS):
                sel = rank2_ref[h, :, lanes] < nbk_ref[h, r:r + 1, lanes]
                gate = g1_ref[h, r:r + 1, lanes] * g2_ref[h, :, lanes]
                coef = coef + jnp.where(sel, gate, 0.0)
            g_scr[rows, lanes] = (coef * a_scr[rows, lanes]).astype(BF16)

    def stage3(c, n):
        rows = slice(c * (d // n), (c + 1) * (d // n))
        o_ref[rows, :] += jnp.dot(vt_ref[rows, :], g_scr[...], preferred_element_type=F32)

    stage1(0, 1)
    for r in range(te // nk):
        stage2(r)
    stage3(0, 1)


def _peer_dense_body(u_ref, vt_ref, xt_ref, rank2_ref, nbk_ref, g1_ref, g2_ref, o_ref, a_scr, g_scr):
    j = pl.program_id(1)

    @pl.when(j == 0)
    def _():
        o_ref[...] = jnp.zeros_like(o_ref)

    _peer_dense_stages(u_ref, vt_ref, xt_ref, rank2_ref, nbk_ref, g1_ref, g2_ref, o_ref, a_scr, g_scr)


def _peer_dense(u, vt, xt, rank2, nbk, g1, g2, t, te):
    e, d = u.shape
    n = xt.shape[1]
    assert n % t == 0 and e % te == 0 and te % PEER_N_KEYS == 0
    kpt = te // PEER_N_KEYS
    sspec = pl.BlockSpec((PEER_HEADS, PEER_N_KEYS, t), lambda i, j: (0, 0, i))
    rspec = pl.BlockSpec((PEER_HEADS, kpt, t), lambda i, j: (0, j, i))
    return pl.pallas_call(
        _peer_dense_body,
        grid=(n // t, e // te),
        in_specs=[pl.BlockSpec((te, d), lambda i, j: (j, 0)),
                  pl.BlockSpec((d, te), lambda i, j: (0, j)),
                  pl.BlockSpec((d, t), lambda i, j: (0, i)),
                  sspec, rspec, rspec, sspec],
        out_specs=pl.BlockSpec((d, t), lambda i, j: (0, i)),
        out_shape=jax.ShapeDtypeStruct((d, n), F32),
        scratch_shapes=[pltpu.VMEM((te, t), F32), pltpu.VMEM((te, t), BF16)],
        compiler_params=_cparams(("arbitrary", "arbitrary")),
        name="peer_dense",
    )(u, vt, xt, rank2, nbk, g1, g2)


def _ln_t_body(ot_ref, r_ref, g_ref, b_ref, y_ref):
    y_ref[...] = _ln(ALPHA * r_ref[...] + ot_ref[...].T, g_ref[...], b_ref[...])


def _ln_transposed(ot, resid, g, b, tm):
    d, n = ot.shape
    assert n % tm == 0
    return pl.pallas_call(
        _ln_t_body,
        grid=(n // tm,),
        in_specs=[pl.BlockSpec((d, tm), lambda i: (0, i)),
                  pl.BlockSpec((tm, d), lambda i: (i, 0)),
                  pl.BlockSpec((1, d), lambda i: (0, 0)),
                  pl.BlockSpec((1, d), lambda i: (0, 0))],
        out_specs=pl.BlockSpec((tm, d), lambda i: (i, 0)),
        out_shape=jax.ShapeDtypeStruct((n, d), F32),
        compiler_params=_cparams(("arbitrary",)),
        name="ln_transposed",
    )(ot, resid, g.reshape(1, d), b.reshape(1, d))


def _layer_tail(h0, cat, attend, wts, tag):
    n = h0.shape[0]
    tm = min(256, n)
    h1 = _matmul_ln(cat, wts["w_out"], h0, wts["ln1_g"], wts["ln1_b"], tm, "out_proj_ln1_" + tag)
    q = _matmul(h1, wts["w_mq"], F32, min(512, n), 1024, "mem_q_" + tag)
    o = attend(q)
    h2, xt = _matmul_ln(o, wts["w_mo"], h1, wts["ln2_g"], wts["ln2_b"], tm, "mem_o_ln2_" + tag,
                        with_transpose=True)
    qt = _matmul(wts["peer_wq_t"], xt, BF16, 512, 512, "peer_q_" + tag)
    rank2, nbk, g1, g2 = _peer_select(qt, wts["subkeys"], V7X_LANES)
    ot = _peer_dense(wts["peer_u"], wts["peer_vt"], xt, rank2, nbk, g1, g2, 512, 1024)
    return _ln_transposed(ot, h2, wts["ln3_g"], wts["ln3_b"], 512)


def kernel(x_prompt, x_sample, mem_prompt, cache_pool, cache_mem_k, cache_mem_v, w_in, pool_w, pool_scale,
           sgu_ln_g, sgu_ln_b, sgu_w, sgu_b, w_out, ln1_g, ln1_b, w_mq, w_mk, w_mv, w_mo, ln2_g, ln2_b,
           peer_wq, peer_subkeys, peer_u, peer_v, ln3_g, ln3_b):
    assert w_in.shape[0] == DEPTH
    l = 0
    batch, seq, d = x_prompt.shape
    dec_b, dec_t, _ = x_sample.shape
    d_pool = pool_scale.shape[1]
    d_sgu = sgu_ln_g.shape[1]
    mem_len = mem_prompt.shape[1]
    mh, mhd = cache_mem_k.shape[3], cache_mem_k.shape[4]

    wts = dict(
        w_out=w_out[l].astype(BF16), ln1_g=ln1_g[l], ln1_b=ln1_b[l],
        w_mq=w_mq[l].astype(BF16), w_mo=w_mo[l].astype(BF16), ln2_g=ln2_g[l], ln2_b=ln2_b[l],
        peer_wq_t=peer_wq[l].astype(BF16).T, subkeys=peer_subkeys[l].astype(BF16),
        peer_u=_cast_table(peer_u, l, 1024, False, "cast_peer_u"),
        peer_vt=_cast_table(peer_v, l, 512, True, "cast_peer_vt"),
        ln3_g=ln3_g[l], ln3_b=ln3_b[l])
    w_in_b = w_in[l].astype(BF16)
    pool_w_b = pool_w[l].astype(BF16)
    hd = d_sgu // SGU_HEADS
    sgu_bias = jnp.repeat(sgu_b[l].T, hd, axis=1)

    hp0 = x_prompt.reshape(batch * seq, d)
    zp = _matmul(hp0, w_in_b, F32, 512, 1024, "in_proj_p")
    cat_p = _mix_prompt(zp, batch, seq, d_pool, d_sgu, pool_w_b, pool_scale[l], sgu_ln_g[l], sgu_ln_b[l],
                        sgu_w[l], sgu_bias)
    pool_p = zp.reshape(batch, seq, -1)[:, seq - POOL_STATE:, :d_pool]
    mem2 = mem_prompt.reshape(batch * mem_len, d)
    mk = _matmul(mem2, w_mk[l].astype(BF16), F32, 512, 1024, "mem_k")
    mv = _matmul(mem2, w_mv[l].astype(BF16), F32, 512, 1024, "mem_v")
    mk3 = mk.reshape(batch, mem_len, d)
    mv3 = mv.reshape(batch, mem_len, d)

    def attend_p(q):
        return _attention(q.reshape(batch, seq, d), mk3, mv3, 512, "attn_p").reshape(batch * seq, d)

    yp = _layer_tail(hp0, cat_p, attend_p, wts, "p")

    hs0 = jnp.transpose(x_sample, (1, 0, 2)).reshape(dec_t * dec_b, d)
    zs = _matmul(hs0, w_in_b, F32, dec_t * dec_b, 1024, "in_proj_s")
    cache_t = jnp.transpose(cache_pool[l], (1, 0, 2))
    wrow = jnp.repeat(jnp.transpose(sgu_w[l][:, :dec_t, :dec_t], (1, 2, 0)), hd, axis=2)
    brow = jnp.repeat(sgu_b[l][:, :dec_t].T, hd, axis=1)
    cat_s, vn_s, pool_s = _mix_sample(zs.reshape(dec_t, dec_b, -1), cache_t, pool_w_b, pool_scale[l],
                                      sgu_ln_g[l], sgu_ln_b[l], wrow, brow, PAST_LEN)
    ck = cache_mem_k[l].reshape(dec_b, mem_len, d)
    cv = cache_mem_v[l].reshape(dec_b, mem_len, d)

    def attend_s(q):
        qb = jnp.transpose(q.reshape(dec_t, dec_b, d), (1, 0, 2))
        ob = _attention(qb, ck, cv, dec_t, "attn_s")
        return jnp.transpose(ob, (1, 0, 2)).reshape(dec_t * dec_b, d)

    ys = _layer_tail(hs0, cat_s.reshape(dec_t * dec_b, -1), attend_s, wts, "s")

    y_prompt = yp.reshape(batch, seq, d)
    y_sample = jnp.transpose(ys.reshape(dec_t, dec_b, d), (1, 0, 2))
    state_pool_prompt = pool_p[None]
    state_mem_k_prompt = mk.reshape(1, batch, mem_len, mh, mhd)
    state_mem_v_prompt = mv.reshape(1, batch, mem_len, mh, mhd)
    state_pool_sample = jnp.transpose(pool_s, (1, 0, 2))[None]
    state_sgu_v_sample = jnp.transpose(vn_s, (1, 0, 2))[None]
    return (y_prompt, y_sample, state_pool_prompt, state_mem_k_prompt, state_mem_v_prompt,
            state_pool_sample, state_sgu_v_sample)
```

```python
import functools
import math

import jax
import jax.numpy as jnp
from jax import lax
from jax.experimental import pallas as pl
from jax.experimental.pallas import tpu as pltpu

F32 = jnp.float32
BF16 = jnp.bfloat16

V7X_LANES = 128
V7X_VMEM_LIMIT_BYTES = 56 * 1024 * 1024

POOL_WINDOWS = (2, 4, 8, 16)
MAX_WIN = max(POOL_WINDOWS)
POOL_STATE = MAX_WIN - 1
SGU_HEADS = 8
CHUNK = 128
MEM_HEADS = 4
PEER_HEADS = 8
PEER_N_KEYS = 128
PEER_TOPK = 16
PAST_LEN = 16384
DEPTH = 1
ALPHA = (2.0 * DEPTH) ** 0.25
LN_EPS = 1e-5

_PAIRS = tuple((a, b) for a in range(PEER_TOPK) for b in range(PEER_TOPK)
               if (a + 1) * (b + 1) <= PEER_TOPK)


def _cparams(semantics=None, flags=None):
    return pltpu.CompilerParams(dimension_semantics=semantics, flags=flags,
                                vmem_limit_bytes=V7X_VMEM_LIMIT_BYTES)


def _ln(x, g, b):
    mu = jnp.mean(x, axis=-1, keepdims=True)
    xc = x - mu
    var = jnp.mean(xc * xc, axis=-1, keepdims=True)
    return xc * lax.rsqrt(var + LN_EPS) * g + b


def _gelu(x):
    return jax.nn.gelu(x, approximate=True)


def _mm_body(x_ref, w_ref, o_ref):
    o_ref[...] = jnp.dot(x_ref[...].astype(BF16), w_ref[...].astype(BF16),
                         preferred_element_type=F32).astype(o_ref.dtype)


def _matmul(x, w, out_dtype, tm, tn, name):
    m, k = x.shape
    _, n = w.shape
    assert m % tm == 0 and n % tn == 0
    return pl.pallas_call(
        _mm_body,
        grid=(m // tm, n // tn),
        in_specs=[pl.BlockSpec((tm, k), lambda i, j: (i, 0)),
                  pl.BlockSpec((k, tn), lambda i, j: (0, j))],
        out_specs=pl.BlockSpec((tm, tn), lambda i, j: (i, j)),
        out_shape=jax.ShapeDtypeStruct((m, n), out_dtype),
        compiler_params=_cparams(("arbitrary", "arbitrary")),
        name=name,
    )(x, w)


def _mm_ln_body(x_ref, w_ref, r_ref, g_ref, b_ref, o_ref, *t_ref):
    m = jnp.dot(x_ref[...].astype(BF16), w_ref[...], preferred_element_type=F32)
    y = _ln(ALPHA * r_ref[...] + m, g_ref[...], b_ref[...])
    o_ref[...] = y
    if t_ref:
        t_ref[0][...] = y.T.astype(t_ref[0].dtype)


def _matmul_ln(x, w, resid, g, b, tm, name, with_transpose=False):
    m, k = x.shape
    _, n = w.shape
    assert m % tm == 0
    out_specs = pl.BlockSpec((tm, n), lambda i: (i, 0))
    out_shape = jax.ShapeDtypeStruct((m, n), F32)
    if with_transpose:
        out_specs = (out_specs, pl.BlockSpec((n, tm), lambda i: (0, i)))
        out_shape = (out_shape, jax.ShapeDtypeStruct((n, m), BF16))
    return pl.pallas_call(
        _mm_ln_body,
        grid=(m // tm,),
        in_specs=[pl.BlockSpec((tm, k), lambda i: (i, 0)),
                  pl.BlockSpec((k, n), lambda i: (0, 0)),
                  pl.BlockSpec((tm, n), lambda i: (i, 0)),
                  pl.BlockSpec((1, n), lambda i: (0, 0)),
                  pl.BlockSpec((1, n), lambda i: (0, 0))],
        out_specs=out_specs,
        out_shape=out_shape,
        compiler_params=_cparams(("arbitrary",)),
        name=name,
    )(x, w, resid, g.reshape(1, n), b.reshape(1, n))


def _cast_table_body(x_ref, o_ref, *, transpose):
    x = x_ref[0]
    o_ref[...] = (x.T if transpose else x).astype(o_ref.dtype)


def _cast_table(w, l, tr, transpose, name):
    _, e, d = w.shape
    assert e % tr == 0
    if transpose:
        out_spec, out_dims = pl.BlockSpec((d, tr), lambda i: (0, i)), (d, e)
    else:
        out_spec, out_dims = pl.BlockSpec((tr, d), lambda i: (i, 0)), (e, d)
    return pl.pallas_call(
        functools.partial(_cast_table_body, transpose=transpose),
        grid=(e // tr,),
        in_specs=[pl.BlockSpec((1, tr, d), lambda i: (l, i, 0))],
        out_specs=out_spec,
        out_shape=jax.ShapeDtypeStruct(out_dims, BF16),
        compiler_params=_cparams(("arbitrary",)),
        name=name,
    )(w)


def _mix_prompt_body(a_ref, ap_ref, u_ref, v_ref, pw_ref, ps_ref, lg_ref, lb_ref, sw_ref, sb_ref,
                     cat_ref, aext_ref):
    c = pl.program_id(1)
    d_pool = a_ref.shape[1]
    grp = d_pool // len(POOL_WINDOWS)
    a = a_ref[...]
    aext_ref[0:MAX_WIN, :] = jnp.where(c > 0, ap_ref[...], 0.0)
    aext_ref[MAX_WIN:MAX_WIN + CHUNK, :] = a
    pos = lax.broadcasted_iota(jnp.int32, (CHUNK, grp), 0) + c * CHUNK
    for g, w in enumerate(POOL_WINDOWS):
        cols = slice(g * grp, (g + 1) * grp)
        s = aext_ref[MAX_WIN:MAX_WIN + CHUNK, cols]
        for k in range(1, w):
            s = s + aext_ref[MAX_WIN - k:MAX_WIN - k + CHUNK, cols]
        cnt = jnp.minimum(pos + 1, w).astype(F32)
        d = s / cnt - a[:, cols]
        y = jnp.dot(d.astype(BF16), pw_ref[g], preferred_element_type=F32) * ps_ref[:, cols]
        cat_ref[:, cols] = y.astype(cat_ref.dtype)
    u = _gelu(u_ref[...])
    vn = _ln(_gelu(v_ref[...]), lg_ref[...], lb_ref[...])
    hd = u.shape[1] // SGU_HEADS
    tri = (lax.broadcasted_iota(jnp.int32, (CHUNK, CHUNK), 0)
           >= lax.broadcasted_iota(jnp.int32, (CHUNK, CHUNK), 1))
    for h in range(SGU_HEADS):
        cols = slice(h * hd, (h + 1) * hd)
        wm = jnp.where(tri, sw_ref[h], 0.0).astype(BF16)
        s = jnp.dot(wm, vn[:, cols].astype(BF16), preferred_element_type=F32) + sb_ref[:, cols]
        cat_ref[:, d_pool + h * hd:d_pool + (h + 1) * hd] = (u[:, cols] * s).astype(cat_ref.dtype)


def _mix_prompt(z, batch, seq, d_pool, d_sgu, pool_w, pool_scale, ln_g, ln_b, sgu_w, sgu_bias):
    assert d_pool == d_sgu and seq % CHUNK == 0
    nch = seq // CHUNK
    sub = CHUNK // MAX_WIN
    row = lambda b, c: b * nch + c
    return pl.pallas_call(
        _mix_prompt_body,
        grid=(batch, nch),
        in_specs=[
            pl.BlockSpec((CHUNK, d_pool), lambda b, c: (row(b, c), 0)),
            pl.BlockSpec((MAX_WIN, d_pool), lambda b, c: (jnp.maximum(row(b, c) * sub - 1, 0), 0)),
            pl.BlockSpec((CHUNK, d_sgu), lambda b, c: (row(b, c), 1)),
            pl.BlockSpec((CHUNK, d_sgu), lambda b, c: (row(b, c), 2)),
            pl.BlockSpec(pool_w.shape, lambda b, c: (0, 0, 0)),
            pl.BlockSpec((1, d_pool), lambda b, c: (0, 0)),
            pl.BlockSpec((1, d_sgu), lambda b, c: (0, 0)),
            pl.BlockSpec((1, d_sgu), lambda b, c: (0, 0)),
            pl.BlockSpec(sgu_w.shape, lambda b, c: (0, 0, 0)),
            pl.BlockSpec((CHUNK, d_sgu), lambda b, c: (0, 0)),
        ],
        out_specs=pl.BlockSpec((CHUNK, d_pool + d_sgu), lambda b, c: (row(b, c), 0)),
        out_shape=jax.ShapeDtypeStruct((batch * seq, d_pool + d_sgu), BF16),
        scratch_shapes=[pltpu.VMEM((MAX_WIN + CHUNK, d_pool), F32)],
        compiler_params=_cparams(("arbitrary", "arbitrary")),
        name="mix_prompt",
    )(z, z, z, z, pool_w, pool_scale.reshape(1, d_pool), ln_g.reshape(1, d_sgu),
      ln_b.reshape(1, d_sgu), sgu_w, sgu_bias)


def _mix_sample_body(z_ref, cache_ref, pw_ref, ps_ref, lg_ref, lb_ref, wrow_ref, brow_ref,
                     cat_ref, vn_ref, st_ref, aext_ref, *, start_pos):
    t_new = z_ref.shape[0]
    d_pool = cache_ref.shape[2]
    d_sgu = vn_ref.shape[2]
    grp = d_pool // len(POOL_WINDOWS)
    aext_ref[0:POOL_STATE] = cache_ref[...]
    for t in range(t_new):
        aext_ref[POOL_STATE + t] = z_ref[t, :, 0:d_pool]
    st_ref[...] = aext_ref[t_new:t_new + POOL_STATE]
    for t in range(t_new):
        a_t = aext_ref[POOL_STATE + t]
        for g, w in enumerate(POOL_WINDOWS):
            cols = slice(g * grp, (g + 1) * grp)
            s = aext_ref[POOL_STATE + t, :, cols]
            for k in range(1, w):
                s = s + aext_ref[POOL_STATE + t - k, :, cols]
            cnt = float(min(start_pos + t + 1, w))
            d = s / cnt - a_t[:, cols]
            y = jnp.dot(d.astype(BF16), pw_ref[g], preferred_element_type=F32) * ps_ref[:, cols]
            cat_ref[t, :, cols] = y.astype(cat_ref.dtype)
    for t in range(t_new):
        vn_ref[t] = _ln(_gelu(z_ref[t, :, d_pool + d_sgu:d_pool + 2 * d_sgu]), lg_ref[...], lb_ref[...])
    for t in range(t_new):
        s = brow_ref[t:t + 1, :]
        for j in range(t + 1):
            s = s + wrow_ref[t, j:j + 1, :] * vn_ref[j]
        u = _gelu(z_ref[t, :, d_pool:d_pool + d_sgu])
        cat_ref[t, :, d_pool:d_pool + d_sgu] = (u * s).astype(cat_ref.dtype)


def _mix_sample(z3, cache_t, pool_w, pool_scale, ln_g, ln_b, wrow, brow, start_pos):
    t_new, nb, d_in = z3.shape
    d_pool = cache_t.shape[2]
    d_sgu = (d_in - d_pool) // 2
    return pl.pallas_call(
        functools.partial(_mix_sample_body, start_pos=start_pos),
        out_shape=(jax.ShapeDtypeStruct((t_new, nb, d_pool + d_sgu), BF16),
                   jax.ShapeDtypeStruct((t_new, nb, d_sgu), F32),
                   jax.ShapeDtypeStruct((POOL_STATE, nb, d_pool), F32)),
        scratch_shapes=[pltpu.VMEM((POOL_STATE + t_new, nb, d_pool), F32)],
        compiler_params=_cparams(),
        name="mix_sample",
    )(z3, cache_t, pool_w, pool_scale.reshape(1, d_pool), ln_g.reshape(1, d_sgu),
      ln_b.reshape(1, d_sgu), wrow, brow)


def _attn_body(q_ref, k_ref, v_ref, o_ref):
    hd = q_ref.shape[2] // MEM_HEADS
    scale = hd ** -0.5
    for h in range(MEM_HEADS):
        cols = slice(h * hd, (h + 1) * hd)
        qh = q_ref[0, :, cols].astype(BF16)
        kh = k_ref[0, :, cols].astype(BF16)
        s = lax.dot_general(qh, kh, (((1,), (1,)), ((), ())), preferred_element_type=F32) * scale
        e = jnp.exp(s - jnp.max(s, axis=-1, keepdims=True))
        p = e / jnp.sum(e, axis=-1, keepdims=True)
        o = jnp.dot(p.astype(BF16), v_ref[0, :, cols].astype(BF16), preferred_element_type=F32)
        o_ref[0, :, cols] = o.astype(o_ref.dtype)


def _attention(q, k, v, tq, name):
    nb, t, d = q.shape
    m = k.shape[1]
    assert t % tq == 0
    return pl.pallas_call(
        _attn_body,
        grid=(nb, t // tq),
        in_specs=[pl.BlockSpec((1, tq, d), lambda b, i: (b, i, 0)),
                  pl.BlockSpec((1, m, d), lambda b, i: (b, 0, 0)),
                  pl.BlockSpec((1, m, d), lambda b, i: (b, 0, 0))],
        out_specs=pl.BlockSpec((1, tq, d), lambda b, i: (b, i, 0)),
        out_shape=jax.ShapeDtypeStruct((nb, t, d), BF16),
        compiler_params=_cparams(("arbitrary", "arbitrary")),
        name=name,
    )(q, k, v)


def _attn_cache_body(q_ref, k_hbm, v_hbm, o_ref, kbuf, vbuf, sem, *, layer):
    b = pl.program_id(0)
    nb = pl.num_programs(0)
    nh, hd = kbuf.shape[1], kbuf.shape[3]
    scale = hd ** -0.5

    def copies(bb, slot):
        out = []
        for h in range(nh):
            out.append(pltpu.make_async_copy(k_hbm.at[layer, bb, :, h, :], kbuf.at[slot, h], sem.at[slot, 0, h]))
            out.append(pltpu.make_async_copy(v_hbm.at[layer, bb, :, h, :], vbuf.at[slot, h], sem.at[slot, 1, h]))
        return out

    @pl.when(b == 0)
    def _():
        for c in copies(0, 0):
            c.start()

    slot = b % 2

    @pl.when(b + 1 < nb)
    def _():
        for c in copies(b + 1, 1 - slot):
            c.start()

    for c in copies(b, slot):
        c.wait()
    for h in range(nh):
        cols = slice(h * hd, (h + 1) * hd)
        qh = q_ref[0, :, cols].astype(BF16)
        s = lax.dot_general(qh, kbuf[slot, h].astype(BF16), (((1,), (1,)), ((), ())),
                            preferred_element_type=F32) * scale
        e = jnp.exp(s - jnp.max(s, axis=-1, keepdims=True))
        p = e / jnp.sum(e, axis=-1, keepdims=True)
        o = jnp.dot(p.astype(BF16), vbuf[slot, h].astype(BF16), preferred_element_type=F32)
        o_ref[0, :, cols] = o.astype(o_ref.dtype)


def _attention_cache(q, k5, v5, layer, name):
    nb, t, d = q.shape
    _, _, m, nh, hd = k5.shape
    assert nh * hd == d and nh == MEM_HEADS
    return pl.pallas_call(
        functools.partial(_attn_cache_body, layer=layer),
        grid=(nb,),
        in_specs=[pl.BlockSpec((1, t, d), lambda b: (b, 0, 0)),
                  pl.BlockSpec(memory_space=pl.ANY),
                  pl.BlockSpec(memory_space=pl.ANY)],
        out_specs=pl.BlockSpec((1, t, d), lambda b: (b, 0, 0)),
        out_shape=jax.ShapeDtypeStruct((nb, t, d), BF16),
        scratch_shapes=[pltpu.VMEM((2, nh, m, hd), F32), pltpu.VMEM((2, nh, m, hd), F32),
                        pltpu.SemaphoreType.DMA((2, 2, nh))],
        compiler_params=_cparams(("arbitrary",)),
        name=name,
    )(q, k5, v5)


def _peer_select_body(qt_ref, sk_ref, rank2_ref, nbk_ref, g1_ref, g2_ref,
                      s_scr, rank_scr, sv_scr, nb_scr, z_scr):
    nk = PEER_N_KEYS
    assert qt_ref.shape[1] == V7X_LANES
    neg = float("-inf")
    kio = lax.broadcasted_iota(jnp.int32, (nk, V7X_LANES), 0).astype(F32)
    for h in range(PEER_HEADS):
        for p in range(2):
            row0 = h * (2 * nk) + p * nk
            s = jnp.dot(sk_ref[p], qt_ref[row0:row0 + nk, :], preferred_element_type=F32)
            s_scr[2 * h + p] = s
            rem = s
            rank = jnp.full((nk, V7X_LANES), float(PEER_TOPK), F32)
            for r in range(PEER_TOPK):
                m = jnp.max(rem, axis=0, keepdims=True)
                first = jnp.min(jnp.where(rem == m, kio, float(nk)), axis=0, keepdims=True)
                hit = kio == first
                rank = jnp.where(hit, float(r), rank)
                rem = jnp.where(hit, neg, rem)
                sv_scr[p, r, h:h + 1, :] = m
            rank_scr[2 * h + p] = rank

    sv1 = [sv_scr[0, a] for a in range(PEER_TOPK)]
    sv2 = [sv_scr[1, b] for b in range(PEER_TOPK)]
    cand = {c: sv1[c[0]] + sv2[c[1]] for c in _PAIRS}
    rk = {c: jnp.full((PEER_HEADS, V7X_LANES), float(i), F32) for i, c in enumerate(_PAIRS)}
    for i, c in enumerate(_PAIRS):
        for c2 in _PAIRS[i + 1:]:
            if c2[0] >= c[0] and c2[1] >= c[1]:
                continue
            later_wins = jnp.where(cand[c2] > cand[c], 1.0, 0.0)
            rk[c] = rk[c] + later_wins
            rk[c2] = rk[c2] - later_wins
    mx = cand[(0, 0)]
    zsum = jnp.zeros((PEER_HEADS, V7X_LANES), F32)
    nb = [jnp.zeros((PEER_HEADS, V7X_LANES), F32) for _ in range(PEER_TOPK)]
    for c in _PAIRS:
        sel = rk[c] < float(PEER_TOPK)
        nb[c[0]] = nb[c[0]] + jnp.where(sel, 1.0, 0.0)
        zsum = zsum + jnp.where(sel, jnp.exp(cand[c] - mx), 0.0)
    for a in range(PEER_TOPK):
        nb_scr[a] = nb[a]
    z_scr[...] = zsum

    for h in range(PEER_HEADS):
        r1 = rank_scr[2 * h]
        nbk = jnp.zeros((nk, V7X_LANES), F32)
        for a in range(PEER_TOPK):
            nbk = jnp.where(r1 == float(a), nb_scr[a, h:h + 1, :], nbk)
        nbk_ref[h] = nbk
        rank2_ref[h] = rank_scr[2 * h + 1]
        g1_ref[h] = jnp.exp(s_scr[2 * h] - sv_scr[0, 0, h:h + 1, :])
        g2_ref[h] = jnp.exp(s_scr[2 * h + 1] - sv_scr[1, 0, h:h + 1, :]) / z_scr[h:h + 1, :]


def _peer_select(qt, subkeys, tl):
    rows, n = qt.shape
    assert rows == PEER_HEADS * 2 * PEER_N_KEYS and n % tl == 0
    out = jax.ShapeDtypeStruct((PEER_HEADS, PEER_N_KEYS, n), F32)
    ospec = pl.BlockSpec((PEER_HEADS, PEER_N_KEYS, tl), lambda i: (0, 0, i))
    return pl.pallas_call(
        _peer_select_body,
        grid=(n // tl,),
        in_specs=[pl.BlockSpec((rows, tl), lambda i: (0, i)),
                  pl.BlockSpec(subkeys.shape, lambda i: (0, 0, 0))],
        out_specs=(ospec, ospec, ospec, ospec),
        out_shape=(out, out, out, out),
        scratch_shapes=[pltpu.VMEM((2 * PEER_HEADS, PEER_N_KEYS, V7X_LANES), F32),
                        pltpu.VMEM((2 * PEER_HEADS, PEER_N_KEYS, V7X_LANES), F32),
                        pltpu.VMEM((2, PEER_TOPK, PEER_HEADS, V7X_LANES), F32),
                        pltpu.VMEM((PEER_TOPK, PEER_HEADS, V7X_LANES), F32),
                        pltpu.VMEM((PEER_HEADS, V7X_LANES), F32)],
        compiler_params=_cparams(("arbitrary",)),
        name="peer_select",
    )(qt, subkeys)


def _peer_dense_stages(u_ref, vt_ref, xt_ref, rank2_ref, nbk_ref, g1_ref, g2_ref, o_ref, a_scr, g_scr):
    te = u_ref.shape[0]
    d = u_ref.shape[1]
    t = xt_ref.shape[1]
    nk = PEER_N_KEYS

    def stage1(c, n):
        rows = slice(c * (te // n), (c + 1) * (te // n))
        a_scr[rows, :] = _gelu(jnp.dot(u_ref[rows, :], xt_ref[...], preferred_element_type=F32))

    def stage2(r):
        rows = slice(r * nk, (r + 1) * nk)
        for lc in range(t // V7X_LANES):
            lanes = slice(lc * V7X_LANES, (lc + 1) * V7X_LANES)
            coef = jnp.zeros((nk, V7X_LANES), F32)
            for h in range(PEER_HEADS):
                sel = rank2_ref[h, :, lanes] < nbk_ref[h, r:r + 1, lanes]
                gate = g1_ref[h, r:r + 1, lanes] * g2_ref[h, :, lanes]
                coef = coef + jnp.where(sel, gate, 0.0)
            g_scr[rows, lanes] = (coef * a_scr[rows, lanes]).astype(BF16)

    def stage3(c, n):
        rows = slice(c * (d // n), (c + 1) * (d // n))
        o_ref[rows, :] += jnp.dot(vt_ref[rows, :], g_scr[...], preferred_element_type=F32)

    stage1(0, 1)
    for r in range(te // nk):
        stage2(r)
    stage3(0, 1)


def _peer_dense_body(u_ref, vt_ref, xt_ref, rank2_ref, nbk_ref, g1_ref, g2_ref, o_ref, a_scr, g_scr):
    j = pl.program_id(1)

    @pl.when(j == 0)
    def _():
        o_ref[...] = jnp.zeros_like(o_ref)

    _peer_dense_stages(u_ref, vt_ref, xt_ref, rank2_ref, nbk_ref, g1_ref, g2_ref, o_ref, a_scr, g_scr)


def _peer_dense(u, vt, xt, rank2, nbk, g1, g2, t, te):
    e, d = u.shape
    n = xt.shape[1]
    assert n % t == 0 and e % te == 0 and te % PEER_N_KEYS == 0
    kpt = te // PEER_N_KEYS
    sspec = pl.BlockSpec((PEER_HEADS, PEER_N_KEYS, t), lambda i, j: (0, 0, i))
    rspec = pl.BlockSpec((PEER_HEADS, kpt, t), lambda i, j: (0, j, i))
    return pl.pallas_call(
        _peer_dense_body,
        grid=(n // t, e // te),
        in_specs=[pl.BlockSpec((te, d), lambda i, j: (j, 0)),
                  pl.BlockSpec((d, te), lambda i, j: (0, j)),
                  pl.BlockSpec((d, t), lambda i, j: (0, i)),
                  sspec, rspec, rspec, sspec],
        out_specs=pl.BlockSpec((d, t), lambda i, j: (0, i)),
        out_shape=jax.ShapeDtypeStruct((d, n), F32),
        scratch_shapes=[pltpu.VMEM((te, t), F32), pltpu.VMEM((te, t), BF16)],
        compiler_params=_cparams(("arbitrary", "arbitrary")),
        name="peer_dense",
    )(u, vt, xt, rank2, nbk, g1, g2)


def _ln_t_body(ot_ref, r_ref, g_ref, b_ref, y_ref):
    y_ref[...] = _ln(ALPHA * r_ref[...] + ot_ref[...].T, g_ref[...], b_ref[...])


def _ln_transposed(ot, resid, g, b, tm):
    d, n = ot.shape
    assert n % tm == 0
    return pl.pallas_call(
        _ln_t_body,
        grid=(n // tm,),
        in_specs=[pl.BlockSpec((d, tm), lambda i: (0, i)),
                  pl.BlockSpec((tm, d), lambda i: (i, 0)),
                  pl.BlockSpec((1, d), lambda i: (0, 0)),
                  pl.BlockSpec((1, d), lambda i: (0, 0))],
        out_specs=pl.BlockSpec((tm, d), lambda i: (i, 0)),
        out_shape=jax.ShapeDtypeStruct((n, d), F32),
        compiler_params=_cparams(("arbitrary",)),
        name="ln_transposed",
    )(ot, resid, g.reshape(1, d), b.reshape(1, d))


def _layer_tail(h0, cat, attend, wts, tag):
    n = h0.shape[0]
    tm = min(256, n)
    h1 = _matmul_ln(cat, wts["w_out"], h0, wts["ln1_g"], wts["ln1_b"], tm, "out_proj_ln1_" + tag)
    q = _matmul(h1, wts["w_mq"], F32, min(512, n), 1024, "mem_q_" + tag)
    o = attend(q)
    h2, xt = _matmul_ln(o, wts["w_mo"], h1, wts["ln2_g"], wts["ln2_b"], tm, "mem_o_ln2_" + tag,
                        with_transpose=True)
    qt = _matmul(wts["peer_wq_t"], xt, BF16, 512, 512, "peer_q_" + tag)
    rank2, nbk, g1, g2 = _peer_select(qt, wts["subkeys"], V7X_LANES)
    ot = _peer_dense(wts["peer_u"], wts["peer_vt"], xt, rank2, nbk, g1, g2, 512, 1024)
    return _ln_transposed(ot, h2, wts["ln3_g"], wts["ln3_b"], 512)


def kernel(x_prompt, x_sample, mem_prompt, cache_pool, cache_mem_k, cache_mem_v, w_in, pool_w, pool_scale,
           sgu_ln_g, sgu_ln_b, sgu_w, sgu_b, w_out, ln1_g, ln1_b, w_mq, w_mk, w_mv, w_mo, ln2_g, ln2_b,
           peer_wq, peer_subkeys, peer_u, peer_v, ln3_g, ln3_b):
    assert w_in.shape[0] == DEPTH
    l = 0
    batch, seq, d = x_prompt.shape
    dec_b, dec_t, _ = x_sample.shape
    d_pool = pool_scale.shape[1]
    d_sgu = sgu_ln_g.shape[1]
    mem_len = mem_prompt.shape[1]
    mh, mhd = cache_mem_k.shape[3], cache_mem_k.shape[4]

    wts = dict(
        w_out=w_out[l].astype(BF16), ln1_g=ln1_g[l], ln1_b=ln1_b[l],
        w_mq=w_mq[l].astype(BF16), w_mo=w_mo[l].astype(BF16), ln2_g=ln2_g[l], ln2_b=ln2_b[l],
        peer_wq_t=peer_wq[l].astype(BF16).T, subkeys=peer_subkeys[l].astype(BF16),
        peer_u=_cast_table(peer_u, l, 1024, False, "cast_peer_u"),
        peer_vt=_cast_table(peer_v, l, 512, True, "cast_peer_vt"),
        ln3_g=ln3_g[l], ln3_b=ln3_b[l])
    w_in_b = w_in[l].astype(BF16)
    pool_w_b = pool_w[l].astype(BF16)
    hd = d_sgu // SGU_HEADS
    sgu_bias = jnp.repeat(sgu_b[l].T, hd, axis=1)

    hp0 = x_prompt.reshape(batch * seq, d)
    zp = _matmul(hp0, w_in_b, F32, 512, 1024, "in_proj_p")
    cat_p = _mix_prompt(zp, batch, seq, d_pool, d_sgu, pool_w_b, pool_scale[l], sgu_ln_g[l], sgu_ln_b[l],
                        sgu_w[l], sgu_bias)
    pool_p = zp.reshape(batch, seq, -1)[:, seq - POOL_STATE:, :d_pool]
    mem2 = mem_prompt.reshape(batch * mem_len, d)
    mk = _matmul(mem2, w_mk[l].astype(BF16), F32, 512, 1024, "mem_k")
    mv = _matmul(mem2, w_mv[l].astype(BF16), F32, 512, 1024, "mem_v")
    mk3 = mk.reshape(batch, mem_len, d)
    mv3 = mv.reshape(batch, mem_len, d)

    def attend_p(q):
        return _attention(q.reshape(batch, seq, d), mk3, mv3, 512, "attn_p").reshape(batch * seq, d)

    yp = _layer_tail(hp0, cat_p, attend_p, wts, "p")

    hs0 = jnp.transpose(x_sample, (1, 0, 2)).reshape(dec_t * dec_b, d)
    zs = _matmul(hs0, w_in_b, F32, dec_t * dec_b, 1024, "in_proj_s")
    cache_t = jnp.transpose(cache_pool[l], (1, 0, 2))
    wrow = jnp.repeat(jnp.transpose(sgu_w[l][:, :dec_t, :dec_t], (1, 2, 0)), hd, axis=2)
    brow = jnp.repeat(sgu_b[l][:, :dec_t].T, hd, axis=1)
    cat_s, vn_s, pool_s = _mix_sample(zs.reshape(dec_t, dec_b, -1), cache_t, pool_w_b, pool_scale[l],
                                      sgu_ln_g[l], sgu_ln_b[l], wrow, brow, PAST_LEN)
    def attend_s(q):
        qb = jnp.transpose(q.reshape(dec_t, dec_b, d), (1, 0, 2))
        ob = _attention_cache(qb, cache_mem_k, cache_mem_v, l, "attn_s")
        return jnp.transpose(ob, (1, 0, 2)).reshape(dec_t * dec_b, d)

    ys = _layer_tail(hs0, cat_s.reshape(dec_t * dec_b, -1), attend_s, wts, "s")

    y_prompt = yp.reshape(batch, seq, d)
    y_sample = jnp.transpose(ys.reshape(dec_t, dec_b, d), (1, 0, 2))
    state_pool_prompt = pool_p[None]
    state_mem_k_prompt = mk.reshape(1, batch, mem_len, mh, mhd)
    state_mem_v_prompt = mv.reshape(1, batch, mem_len, mh, mhd)
    state_pool_sample = jnp.transpose(pool_s, (1, 0, 2))[None]
    state_sgu_v_sample = jnp.transpose(vn_s, (1, 0, 2))[None]
    return (y_prompt, y_sample, state_pool_prompt, state_mem_k_prompt, state_mem_v_prompt,
            state_pool_sample, state_sgu_v_sample)
```

```python
import functools
import math

import jax
import jax.numpy as jnp
from jax import lax
from jax.experimental import pallas as pl
from jax.experimental.pallas import tpu as pltpu

F32 = jnp.float32
BF16 = jnp.bfloat16

V7X_LANES = 128
BF16_SUBLANES = 16
V7X_VMEM_LIMIT_BYTES = 56 * 1024 * 1024

POOL_WINDOWS = (2, 4, 8, 16)
MAX_WIN = max(POOL_WINDOWS)
POOL_STATE = MAX_WIN - 1
SGU_HEADS = 8
CHUNK = 128
MEM_HEADS = 4
PEER_HEADS = 8
PEER_N_KEYS = 128
PEER_TOPK = 16
PAST_LEN = 16384
DEPTH = 1
ALPHA = (2.0 * DEPTH) ** 0.25
LN_EPS = 1e-5

_PAIRS = tuple((a, b) for a in range(PEER_TOPK) for b in range(PEER_TOPK)
               if (a + 1) * (b + 1) <= PEER_TOPK)


def _cparams(semantics=None, flags=None):
    return pltpu.CompilerParams(dimension_semantics=semantics, flags=flags,
                                vmem_limit_bytes=V7X_VMEM_LIMIT_BYTES)


def _ln(x, g, b):
    mu = jnp.mean(x, axis=-1, keepdims=True)
    xc = x - mu
    var = jnp.mean(xc * xc, axis=-1, keepdims=True)
    return xc * lax.rsqrt(var + LN_EPS) * g + b


def _gelu(x):
    return jax.nn.gelu(x, approximate=True)


def _mm_body(x_ref, w_ref, o_ref):
    o_ref[...] = jnp.dot(x_ref[...].astype(BF16), w_ref[...].astype(BF16),
                         preferred_element_type=F32).astype(o_ref.dtype)


def _matmul(x, w, out_dtype, tm, tn, name):
    m, k = x.shape
    _, n = w.shape
    assert m % tm == 0 and n % tn == 0
    return pl.pallas_call(
        _mm_body,
        grid=(m // tm, n // tn),
        in_specs=[pl.BlockSpec((tm, k), lambda i, j: (i, 0)),
                  pl.BlockSpec((k, tn), lambda i, j: (0, j))],
        out_specs=pl.BlockSpec((tm, tn), lambda i, j: (i, j)),
        out_shape=jax.ShapeDtypeStruct((m, n), out_dtype),
        compiler_params=_cparams(("arbitrary", "arbitrary")),
        name=name,
    )(x, w)


def _mm_ln_body(x_ref, w_ref, r_ref, g_ref, b_ref, o_ref, *t_ref):
    m = jnp.dot(x_ref[...].astype(BF16), w_ref[...], preferred_element_type=F32)
    y = _ln(ALPHA * r_ref[...] + m, g_ref[...], b_ref[...])
    o_ref[...] = y
    if t_ref:
        t_ref[0][...] = y.T.astype(t_ref[0].dtype)


def _matmul_ln(x, w, resid, g, b, tm, name, with_transpose=False):
    m, k = x.shape
    _, n = w.shape
    assert m % tm == 0
    out_specs = pl.BlockSpec((tm, n), lambda i: (i, 0))
    out_shape = jax.ShapeDtypeStruct((m, n), F32)
    if with_transpose:
        out_specs = (out_specs, pl.BlockSpec((n, tm), lambda i: (0, i)))
        out_shape = (out_shape, jax.ShapeDtypeStruct((n, m), BF16))
    return pl.pallas_call(
        _mm_ln_body,
        grid=(m // tm,),
        in_specs=[pl.BlockSpec((tm, k), lambda i: (i, 0)),
                  pl.BlockSpec((k, n), lambda i: (0, 0), pipeline_mode=pl.Buffered(1)),
                  pl.BlockSpec((tm, n), lambda i: (i, 0)),
                  pl.BlockSpec((1, n), lambda i: (0, 0)),
                  pl.BlockSpec((1, n), lambda i: (0, 0))],
        out_specs=out_specs,
        out_shape=out_shape,
        compiler_params=_cparams(("arbitrary",)),
        name=name,
    )(x, w, resid, g.reshape(1, n), b.reshape(1, n))


def _cast_table_body(x_ref, o_ref, *, transpose):
    x = x_ref[0]
    o_ref[...] = (x.T if transpose else x).astype(o_ref.dtype)


def _cast_table(w, l, tr, transpose, name):
    _, e, d = w.shape
    assert e % tr == 0
    if transpose:
        out_spec, out_dims = pl.BlockSpec((d, tr), lambda i: (0, i)), (d, e)
    else:
        out_spec, out_dims = pl.BlockSpec((tr, d), lambda i: (i, 0)), (e, d)
    return pl.pallas_call(
        functools.partial(_cast_table_body, transpose=transpose),
        grid=(e // tr,),
        in_specs=[pl.BlockSpec((1, tr, d), lambda i: (l, i, 0))],
        out_specs=out_spec,
        out_shape=jax.ShapeDtypeStruct(out_dims, BF16),
        compiler_params=_cparams(("arbitrary",)),
        name=name,
    )(w)


def _mix_prompt_body(a_ref, ap_ref, u_ref, v_ref, pw_ref, ps_ref, lg_ref, lb_ref, sw_ref, sb_ref,
                     cat_ref, aext_ref):
    c = pl.program_id(1)
    d_pool = a_ref.shape[1]
    grp = d_pool // len(POOL_WINDOWS)
    a = a_ref[...]
    aext_ref[0:MAX_WIN, :] = jnp.where(c > 0, ap_ref[...], 0.0)
    aext_ref[MAX_WIN:MAX_WIN + CHUNK, :] = a
    pos = lax.broadcasted_iota(jnp.int32, (CHUNK, grp), 0) + c * CHUNK
    for g, w in enumerate(POOL_WINDOWS):
        cols = slice(g * grp, (g + 1) * grp)
        s = aext_ref[MAX_WIN:MAX_WIN + CHUNK, cols]
        for k in range(1, w):
            s = s + aext_ref[MAX_WIN - k:MAX_WIN - k + CHUNK, cols]
        cnt = jnp.minimum(pos + 1, w).astype(F32)
        d = s / cnt - a[:, cols]
        y = jnp.dot(d.astype(BF16), pw_ref[g], preferred_element_type=F32) * ps_ref[:, cols]
        cat_ref[:, cols] = y.astype(cat_ref.dtype)
    u = _gelu(u_ref[...])
    vn = _ln(_gelu(v_ref[...]), lg_ref[...], lb_ref[...])
    hd = u.shape[1] // SGU_HEADS
    tri = (lax.broadcasted_iota(jnp.int32, (CHUNK, CHUNK), 0)
           >= lax.broadcasted_iota(jnp.int32, (CHUNK, CHUNK), 1))
    for h in range(SGU_HEADS):
        cols = slice(h * hd, (h + 1) * hd)
        wm = jnp.where(tri, sw_ref[h], 0.0).astype(BF16)
        s = jnp.dot(wm, vn[:, cols].astype(BF16), preferred_element_type=F32) + sb_ref[:, cols]
        cat_ref[:, d_pool + h * hd:d_pool + (h + 1) * hd] = (u[:, cols] * s).astype(cat_ref.dtype)


def _mix_prompt(z, batch, seq, d_pool, d_sgu, pool_w, pool_scale, ln_g, ln_b, sgu_w, sgu_bias):
    assert d_pool == d_sgu and seq % CHUNK == 0
    nch = seq // CHUNK
    sub = CHUNK // MAX_WIN
    row = lambda b, c: b * nch + c
    return pl.pallas_call(
        _mix_prompt_body,
        grid=(batch, nch),
        in_specs=[
            pl.BlockSpec((CHUNK, d_pool), lambda b, c: (row(b, c), 0)),
            pl.BlockSpec((MAX_WIN, d_pool), lambda b, c: (jnp.maximum(row(b, c) * sub - 1, 0), 0)),
            pl.BlockSpec((CHUNK, d_sgu), lambda b, c: (row(b, c), 1)),
            pl.BlockSpec((CHUNK, d_sgu), lambda b, c: (row(b, c), 2)),
            pl.BlockSpec(pool_w.shape, lambda b, c: (0, 0, 0)),
            pl.BlockSpec((1, d_pool), lambda b, c: (0, 0)),
            pl.BlockSpec((1, d_sgu), lambda b, c: (0, 0)),
            pl.BlockSpec((1, d_sgu), lambda b, c: (0, 0)),
            pl.BlockSpec(sgu_w.shape, lambda b, c: (0, 0, 0)),
            pl.BlockSpec((CHUNK, d_sgu), lambda b, c: (0, 0)),
        ],
        out_specs=pl.BlockSpec((CHUNK, d_pool + d_sgu), lambda b, c: (row(b, c), 0)),
        out_shape=jax.ShapeDtypeStruct((batch * seq, d_pool + d_sgu), BF16),
        scratch_shapes=[pltpu.VMEM((MAX_WIN + CHUNK, d_pool), F32)],
        compiler_params=_cparams(("arbitrary", "arbitrary")),
        name="mix_prompt",
    )(z, z, z, z, pool_w, pool_scale.reshape(1, d_pool), ln_g.reshape(1, d_sgu),
      ln_b.reshape(1, d_sgu), sgu_w, sgu_bias)


def _mix_sample_body(z_ref, cache_ref, pw_ref, ps_ref, lg_ref, lb_ref, wrow_ref, brow_ref,
                     cat_ref, vn_ref, st_ref, aext_ref, *, start_pos):
    t_new = z_ref.shape[0]
    d_pool = cache_ref.shape[2]
    d_sgu = vn_ref.shape[2]
    grp = d_pool // len(POOL_WINDOWS)
    aext_ref[0:POOL_STATE] = cache_ref[...]
    for t in range(t_new):
        aext_ref[POOL_STATE + t] = z_ref[t, :, 0:d_pool]
    st_ref[...] = aext_ref[t_new:t_new + POOL_STATE]
    for t in range(t_new):
        a_t = aext_ref[POOL_STATE + t]
        for g, w in enumerate(POOL_WINDOWS):
            cols = slice(g * grp, (g + 1) * grp)
            s = aext_ref[POOL_STATE + t, :, cols]
            for k in range(1, w):
                s = s + aext_ref[POOL_STATE + t - k, :, cols]
            cnt = float(min(start_pos + t + 1, w))
            d = s / cnt - a_t[:, cols]
            y = jnp.dot(d.astype(BF16), pw_ref[g], preferred_element_type=F32) * ps_ref[:, cols]
            cat_ref[t, :, cols] = y.astype(cat_ref.dtype)
    for t in range(t_new):
        vn_ref[t] = _ln(_gelu(z_ref[t, :, d_pool + d_sgu:d_pool + 2 * d_sgu]), lg_ref[...], lb_ref[...])
    for t in range(t_new):
        s = brow_ref[t:t + 1, :]
        for j in range(t + 1):
            s = s + wrow_ref[t, j:j + 1, :] * vn_ref[j]
        u = _gelu(z_ref[t, :, d_pool:d_pool + d_sgu])
        cat_ref[t, :, d_pool:d_pool + d_sgu] = (u * s).astype(cat_ref.dtype)


def _mix_sample(z3, cache_t, pool_w, pool_scale, ln_g, ln_b, wrow, brow, start_pos):
    t_new, nb, d_in = z3.shape
    d_pool = cache_t.shape[2]
    d_sgu = (d_in - d_pool) // 2
    return pl.pallas_call(
        functools.partial(_mix_sample_body, start_pos=start_pos),
        out_shape=(jax.ShapeDtypeStruct((t_new, nb, d_pool + d_sgu), BF16),
                   jax.ShapeDtypeStruct((t_new, nb, d_sgu), F32),
                   jax.ShapeDtypeStruct((POOL_STATE, nb, d_pool), F32)),
        scratch_shapes=[pltpu.VMEM((POOL_STATE + t_new, nb, d_pool), F32)],
        compiler_params=_cparams(),
        name="mix_sample",
    )(z3, cache_t, pool_w, pool_scale.reshape(1, d_pool), ln_g.reshape(1, d_sgu),
      ln_b.reshape(1, d_sgu), wrow, brow)


def _attn_body(q_ref, k_ref, v_ref, o_ref):
    hd = q_ref.shape[2] // MEM_HEADS
    scale = hd ** -0.5
    for h in range(MEM_HEADS):
        cols = slice(h * hd, (h + 1) * hd)
        qh = q_ref[0, :, cols].astype(BF16)
        kh = k_ref[0, :, cols].astype(BF16)
        s = lax.dot_general(qh, kh, (((1,), (1,)), ((), ())), preferred_element_type=F32) * scale
        e = jnp.exp(s - jnp.max(s, axis=-1, keepdims=True))
        p = e / jnp.sum(e, axis=-1, keepdims=True)
        o = jnp.dot(p.astype(BF16), v_ref[0, :, cols].astype(BF16), preferred_element_type=F32)
        o_ref[0, :, cols] = o.astype(o_ref.dtype)


def _attention(q, k, v, tq, name):
    nb, t, d = q.shape
    m = k.shape[1]
    assert t % tq == 0
    return pl.pallas_call(
        _attn_body,
        grid=(nb, t // tq),
        in_specs=[pl.BlockSpec((1, tq, d), lambda b, i: (b, i, 0)),
                  pl.BlockSpec((1, m, d), lambda b, i: (b, 0, 0)),
                  pl.BlockSpec((1, m, d), lambda b, i: (b, 0, 0))],
        out_specs=pl.BlockSpec((1, tq, d), lambda b, i: (b, i, 0)),
        out_shape=jax.ShapeDtypeStruct((nb, t, d), BF16),
        compiler_params=_cparams(("arbitrary", "arbitrary")),
        name=name,
    )(q, k, v)


def _attn_cache_body(q_ref, k_hbm, v_hbm, o_ref, kbuf, vbuf, sem, *, layer):
    b = pl.program_id(0)
    nb = pl.num_programs(0)
    nh, hd = kbuf.shape[1], kbuf.shape[3]
    scale = hd ** -0.5

    def copies(bb, slot):
        out = []
        for h in range(nh):
            out.append(pltpu.make_async_copy(k_hbm.at[layer, bb, :, h, :], kbuf.at[slot, h], sem.at[slot, 0, h]))
            out.append(pltpu.make_async_copy(v_hbm.at[layer, bb, :, h, :], vbuf.at[slot, h], sem.at[slot, 1, h]))
        return out

    @pl.when(b == 0)
    def _():
        for c in copies(0, 0):
            c.start()

    slot = b % 2

    @pl.when(b + 1 < nb)
    def _():
        for c in copies(b + 1, 1 - slot):
            c.start()

    for c in copies(b, slot):
        c.wait()
    for h in range(nh):
        cols = slice(h * hd, (h + 1) * hd)
        qh = q_ref[0, :, cols].astype(BF16)
        s = lax.dot_general(qh, kbuf[slot, h].astype(BF16), (((1,), (1,)), ((), ())),
                            preferred_element_type=F32) * scale
        e = jnp.exp(s - jnp.max(s, axis=-1, keepdims=True))
        p = e / jnp.sum(e, axis=-1, keepdims=True)
        o = jnp.dot(p.astype(BF16), vbuf[slot, h].astype(BF16), preferred_element_type=F32)
        o_ref[0, :, cols] = o.astype(o_ref.dtype)


def _attention_cache(q, k5, v5, layer, name):
    nb, t, d = q.shape
    _, _, m, nh, hd = k5.shape
    assert nh * hd == d and nh == MEM_HEADS
    return pl.pallas_call(
        functools.partial(_attn_cache_body, layer=layer),
        grid=(nb,),
        in_specs=[pl.BlockSpec((1, t, d), lambda b: (b, 0, 0)),
                  pl.BlockSpec(memory_space=pl.ANY),
                  pl.BlockSpec(memory_space=pl.ANY)],
        out_specs=pl.BlockSpec((1, t, d), lambda b: (b, 0, 0)),
        out_shape=jax.ShapeDtypeStruct((nb, t, d), BF16),
        scratch_shapes=[pltpu.VMEM((2, nh, m, hd), F32), pltpu.VMEM((2, nh, m, hd), F32),
                        pltpu.SemaphoreType.DMA((2, 2, nh))],
        compiler_params=_cparams(("arbitrary",)),
        name=name,
    )(q, k5, v5)


def _peer_select_body(qt_ref, sk_ref, rank2_ref, nbk_ref, g1_ref, g2_ref,
                      s_scr, rank_scr, sv_scr, nb_scr, z_scr):
    nk = PEER_N_KEYS
    assert qt_ref.shape[1] == V7X_LANES
    neg = float("-inf")
    topk = float(PEER_TOPK)

    tied = jnp.zeros((1, V7X_LANES), F32)
    for h in range(PEER_HEADS):
        for p in range(2):
            row0 = h * (2 * nk) + p * nk
            s = jnp.dot(sk_ref[p], qt_ref[row0:row0 + nk, :], preferred_element_type=F32)
            s_scr[2 * h + p] = s
            rem = s
            rank = jnp.full((nk, V7X_LANES), topk, F32)
            for r in range(PEER_TOPK):
                m = jnp.max(rem, axis=0, keepdims=True)
                hit = rem == m
                rank = jnp.where(hit, float(r), rank)
                rem = jnp.where(hit, neg, rem)
                sv_scr[p, r, h:h + 1, :] = m
            rank_scr[2 * h + p] = rank
            n_ranked = jnp.sum(jnp.where(rank < topk, 1.0, 0.0), axis=0, keepdims=True)
            tied = jnp.maximum(tied, jnp.where(n_ranked != topk, 1.0, 0.0))

    @pl.when(jnp.max(tied) > 0.0)
    def _():
        kio = lax.broadcasted_iota(jnp.int32, (nk, V7X_LANES), 0).astype(F32)
        for h in range(PEER_HEADS):
            for p in range(2):
                rem = s_scr[2 * h + p]
                rank = jnp.full((nk, V7X_LANES), topk, F32)
                for r in range(PEER_TOPK):
                    m = jnp.max(rem, axis=0, keepdims=True)
                    first = jnp.min(jnp.where(rem == m, kio, float(nk)), axis=0, keepdims=True)
                    hit = kio == first
                    rank = jnp.where(hit, float(r), rank)
                    rem = jnp.where(hit, neg, rem)
                    sv_scr[p, r, h:h + 1, :] = m
                rank_scr[2 * h + p] = rank

    sv1 = [sv_scr[0, a] for a in range(PEER_TOPK)]
    sv2 = [sv_scr[1, b] for b in range(PEER_TOPK)]
    cand = {c: sv1[c[0]] + sv2[c[1]] for c in _PAIRS}
    rk = {c: jnp.full((PEER_HEADS, V7X_LANES), float(i), F32) for i, c in enumerate(_PAIRS)}
    for i, c in enumerate(_PAIRS):
        for c2 in _PAIRS[i + 1:]:
            if c2[0] >= c[0] and c2[1] >= c[1]:
                continue
            later_wins = jnp.where(cand[c2] > cand[c], 1.0, 0.0)
            rk[c] = rk[c] + later_wins
            rk[c2] = rk[c2] - later_wins
    mx = cand[(0, 0)]
    zsum = jnp.zeros((PEER_HEADS, V7X_LANES), F32)
    nb = [jnp.zeros((PEER_HEADS, V7X_LANES), F32) for _ in range(PEER_TOPK)]
    for c in _PAIRS:
        sel = rk[c] < float(PEER_TOPK)
        nb[c[0]] = nb[c[0]] + jnp.where(sel, 1.0, 0.0)
        zsum = zsum + jnp.where(sel, jnp.exp(cand[c] - mx), 0.0)
    for a in range(PEER_TOPK):
        nb_scr[a] = nb[a]
    z_scr[...] = zsum

    for h in range(PEER_HEADS):
        r1 = rank_scr[2 * h]
        nbk = jnp.zeros((nk, V7X_LANES), F32)
        for a in range(PEER_TOPK):
            nbk = jnp.where(r1 == float(a), nb_scr[a, h:h + 1, :], nbk)
        nbk_ref[h] = nbk
        rank2_ref[h] = rank_scr[2 * h + 1].astype(rank2_ref.dtype)
        g1_ref[h] = jnp.exp(s_scr[2 * h] - sv_scr[0, 0, h:h + 1, :])
        g2 = jnp.exp(s_scr[2 * h + 1] - sv_scr[1, 0, h:h + 1, :]) / z_scr[h:h + 1, :]
        g2_ref[h] = g2.astype(g2_ref.dtype)


def _peer_select(qt, subkeys, tl):
    rows, n = qt.shape
    assert rows == PEER_HEADS * 2 * PEER_N_KEYS and n % tl == 0
    out = jax.ShapeDtypeStruct((PEER_HEADS, PEER_N_KEYS, n), F32)
    out_b = jax.ShapeDtypeStruct((PEER_HEADS, PEER_N_KEYS, n), BF16)
    ospec = pl.BlockSpec((PEER_HEADS, PEER_N_KEYS, tl), lambda i: (0, 0, i))
    return pl.pallas_call(
        _peer_select_body,
        grid=(n // tl,),
        in_specs=[pl.BlockSpec((rows, tl), lambda i: (0, i)),
                  pl.BlockSpec(subkeys.shape, lambda i: (0, 0, 0))],
        out_specs=(ospec, ospec, ospec, ospec),
        out_shape=(out_b, out, out, out_b),
        scratch_shapes=[pltpu.VMEM((2 * PEER_HEADS, PEER_N_KEYS, V7X_LANES), F32),
                        pltpu.VMEM((2 * PEER_HEADS, PEER_N_KEYS, V7X_LANES), F32),
                        pltpu.VMEM((2, PEER_TOPK, PEER_HEADS, V7X_LANES), F32),
                        pltpu.VMEM((PEER_TOPK, PEER_HEADS, V7X_LANES), F32),
                        pltpu.VMEM((PEER_HEADS, V7X_LANES), F32)],
        compiler_params=_cparams(("arbitrary",)),
        name="peer_select",
    )(qt, subkeys)


def _peer_dense_body(u_ref, vt_ref, xt_ref, rank2_ref, nbk_ref, g1_ref, g2_ref, r_ref, lg_ref, lb_ref,
                     y_ref, o_ref, a_scr, g_scr, r2_scr, g2_scr):
    j = pl.program_id(1)
    te = u_ref.shape[0]
    t = xt_ref.shape[1]
    nk = PEER_N_KEYS
    pk = BF16_SUBLANES
    zero = jnp.zeros((), BF16)

    @pl.when(j == 0)
    def _():
        o_ref[...] = jnp.zeros_like(o_ref)
        r2_scr[...] = rank2_ref[...]
        g2_scr[...] = g2_ref[...]

    a_scr[...] = _gelu(jnp.dot(u_ref[...], xt_ref[...], preferred_element_type=F32)).astype(BF16)
    for r in range(te // nk):
        for lc in range(t // V7X_LANES):
            lanes = slice(lc * V7X_LANES, (lc + 1) * V7X_LANES)
            coef = [jnp.zeros((pk, V7X_LANES), BF16) for _ in range(nk // pk)]
            for h in range(PEER_HEADS):
                nb = jnp.broadcast_to(nbk_ref[h, r:r + 1, lanes], (pk, V7X_LANES)).astype(BF16)
                g1 = jnp.broadcast_to(g1_ref[h, r:r + 1, lanes], (pk, V7X_LANES)).astype(BF16)
                for s in range(nk // pk):
                    sub = slice(s * pk, (s + 1) * pk)
                    sel = r2_scr[h, sub, lanes] < nb
                    coef[s] = coef[s] + jnp.where(sel, g1 * g2_scr[h, sub, lanes], zero)
            for s in range(nk // pk):
                rows = slice(r * nk + s * pk, r * nk + (s + 1) * pk)
                g_scr[rows, lanes] = coef[s] * a_scr[rows, lanes]
    o_ref[...] += jnp.dot(vt_ref[...], g_scr[...], preferred_element_type=F32)

    @pl.when(j == pl.num_programs(1) - 1)
    def _():
        y_ref[...] = _ln(ALPHA * r_ref[...] + o_ref[...].T, lg_ref[...], lb_ref[...])


def _peer_dense(u, vt, xt, rank2, nbk, g1, g2, resid, ln_g, ln_b, t, te):
    e, d = u.shape
    n = xt.shape[1]
    assert n % t == 0 and e % te == 0 and te % PEER_N_KEYS == 0
    kpt = te // PEER_N_KEYS
    sspec = pl.BlockSpec((PEER_HEADS, PEER_N_KEYS, t), lambda i, j: (0, 0, i))
    rspec = pl.BlockSpec((PEER_HEADS, kpt, t), lambda i, j: (0, j, i))
    vspec = pl.BlockSpec((1, d), lambda i, j: (0, 0))
    return pl.pallas_call(
        _peer_dense_body,
        grid=(n // t, e // te),
        in_specs=[pl.BlockSpec((te, d), lambda i, j: (j, 0)),
                  pl.BlockSpec((d, te), lambda i, j: (0, j)),
                  pl.BlockSpec((d, t), lambda i, j: (0, i)),
                  sspec, rspec, rspec, sspec,
                  pl.BlockSpec((t, d), lambda i, j: (i, 0), pipeline_mode=pl.Buffered(1)), vspec, vspec],
        out_specs=pl.BlockSpec((t, d), lambda i, j: (i, 0)),
        out_shape=jax.ShapeDtypeStruct((n, d), F32),
        scratch_shapes=[pltpu.VMEM((d, t), F32), pltpu.VMEM((te, t), BF16), pltpu.VMEM((te, t), BF16),
                        pltpu.VMEM((PEER_HEADS, PEER_N_KEYS, t), BF16),
                        pltpu.VMEM((PEER_HEADS, PEER_N_KEYS, t), BF16)],
        compiler_params=_cparams(("arbitrary", "arbitrary")),
        name="peer_dense",
    )(u, vt, xt, rank2, nbk, g1, g2, resid, ln_g.reshape(1, d), ln_b.reshape(1, d))


def _layer_tail(h0, cat, attend, wts, tag):
    n = h0.shape[0]
    tm = min(512, n)
    h1 = _matmul_ln(cat, wts["w_out"], h0, wts["ln1_g"], wts["ln1_b"], tm, "out_proj_ln1_" + tag)
    q = _matmul(h1, wts["w_mq"], F32, min(1024, n), 1024, "mem_q_" + tag)
    o = attend(q)
    h2, xt = _matmul_ln(o, wts["w_mo"], h1, wts["ln2_g"], wts["ln2_b"], tm, "mem_o_ln2_" + tag,
                        with_transpose=True)
    qt = _matmul(wts["peer_wq_t"], xt, BF16, 1024, min(1024, n), "peer_q_" + tag)
    rank2, nbk, g1, g2 = _peer_select(qt, wts["subkeys"], V7X_LANES)
    return _peer_dense(wts["peer_u"], wts["peer_vt"], xt, rank2, nbk, g1, g2, h2, wts["ln3_g"], wts["ln3_b"],
                       512, 1024)


def kernel(x_prompt, x_sample, mem_prompt, cache_pool, cache_mem_k, cache_mem_v, w_in, pool_w, pool_scale,
           sgu_ln_g, sgu_ln_b, sgu_w, sgu_b, w_out, ln1_g, ln1_b, w_mq, w_mk, w_mv, w_mo, ln2_g, ln2_b,
           peer_wq, peer_subkeys, peer_u, peer_v, ln3_g, ln3_b):
    assert w_in.shape[0] == DEPTH
    l = 0
    batch, seq, d = x_prompt.shape
    dec_b, dec_t, _ = x_sample.shape
    d_pool = pool_scale.shape[1]
    d_sgu = sgu_ln_g.shape[1]
    mem_len = mem_prompt.shape[1]
    mh, mhd = cache_mem_k.shape[3], cache_mem_k.shape[4]

    wts = dict(
        w_out=w_out[l].astype(BF16), ln1_g=ln1_g[l], ln1_b=ln1_b[l],
        w_mq=w_mq[l].astype(BF16), w_mo=w_mo[l].astype(BF16), ln2_g=ln2_g[l], ln2_b=ln2_b[l],
        peer_wq_t=peer_wq[l].astype(BF16).T, subkeys=peer_subkeys[l].astype(BF16),
        peer_u=_cast_table(peer_u, l, 1024, False, "cast_peer_u"),
        peer_vt=_cast_table(peer_v, l, 512, True, "cast_peer_vt"),
        ln3_g=ln3_g[l], ln3_b=ln3_b[l])
    w_in_b = w_in[l].astype(BF16)
    pool_w_b = pool_w[l].astype(BF16)
    hd = d_sgu // SGU_HEADS
    sgu_bias = jnp.repeat(sgu_b[l].T, hd, axis=1)

    hp0 = x_prompt.reshape(batch * seq, d)
    zp = _matmul(hp0, w_in_b, F32, 1024, 1024, "in_proj_p")
    cat_p = _mix_prompt(zp, batch, seq, d_pool, d_sgu, pool_w_b, pool_scale[l], sgu_ln_g[l], sgu_ln_b[l],
                        sgu_w[l], sgu_bias)
    pool_p = zp.reshape(batch, seq, -1)[:, seq - POOL_STATE:, :d_pool]
    mem2 = mem_prompt.reshape(batch * mem_len, d)
    mk = _matmul(mem2, w_mk[l].astype(BF16), F32, 512, 1024, "mem_k")
    mv = _matmul(mem2, w_mv[l].astype(BF16), F32, 512, 1024, "mem_v")
    mk3 = mk.reshape(batch, mem_len, d)
    mv3 = mv.reshape(batch, mem_len, d)

    def attend_p(q):
        return _attention(q.reshape(batch, seq, d), mk3, mv3, 512, "attn_p").reshape(batch * seq, d)

    yp = _layer_tail(hp0, cat_p, attend_p, wts, "p")

    hs0 = jnp.transpose(x_sample, (1, 0, 2)).reshape(dec_t * dec_b, d)
    zs = _matmul(hs0, w_in_b, F32, dec_t * dec_b, 1024, "in_proj_s")
    cache_t = jnp.transpose(cache_pool[l], (1, 0, 2))
    wrow = jnp.repeat(jnp.transpose(sgu_w[l][:, :dec_t, :dec_t], (1, 2, 0)), hd, axis=2)
    brow = jnp.repeat(sgu_b[l][:, :dec_t].T, hd, axis=1)
    cat_s, vn_s, pool_s = _mix_sample(zs.reshape(dec_t, dec_b, -1), cache_t, pool_w_b, pool_scale[l],
                                      sgu_ln_g[l], sgu_ln_b[l], wrow, brow, PAST_LEN)
    def attend_s(q):
        qb = jnp.transpose(q.reshape(dec_t, dec_b, d), (1, 0, 2))
        ob = _attention_cache(qb, cache_mem_k, cache_mem_v, l, "attn_s")
        return jnp.transpose(ob, (1, 0, 2)).reshape(dec_t * dec_b, d)

    ys = _layer_tail(hs0, cat_s.reshape(dec_t * dec_b, -1), attend_s, wts, "s")

    y_prompt = yp.reshape(batch, seq, d)
    y_sample = jnp.transpose(ys.reshape(dec_t, dec_b, d), (1, 0, 2))
    state_pool_prompt = pool_p[None]
    state_mem_k_prompt = mk.reshape(1, batch, mem_len, mh, mhd)
    state_mem_v_prompt = mv.reshape(1, batch, mem_len, mh, mhd)
    state_pool_sample = jnp.transpose(pool_s, (1, 0, 2))[None]
    state_sgu_v_sample = jnp.transpose(vn_s, (1, 0, 2))[None]
    return (y_prompt, y_sample, state_pool_prompt, state_mem_k_prompt, state_mem_v_prompt,
            state_pool_sample, state_sgu_v_sample)
```
